```python
import math
import jax
import jax.numpy as jnp
from jax import lax
import numpy as np

D_MODEL = 1024
BATCH = 16
SEQ = 4096
DEPTH = 4

GRID_W = 64
CTX_LEN = 256
EPS = 1e-6
N_MOD = 6

DN_QK_HEADS = 8
DN_V_HEADS = 16
DN_HEAD_DIM = 128
DN_REP = DN_V_HEADS // DN_QK_HEADS
DN_KEY_W = DN_QK_HEADS * DN_HEAD_DIM
DN_VAL_W = DN_V_HEADS * DN_HEAD_DIM
DN_CONV_W = 2 * DN_KEY_W + DN_VAL_W
DN_IN_W = DN_CONV_W + DN_VAL_W + 4 * DN_V_HEADS
DN_CONV = 5
DN_CHUNK = 64

ATT_Q_HEADS = 8
ATT_KV_HEADS = 2
ATT_GROUP = ATT_Q_HEADS // ATT_KV_HEADS
ATT_HEAD_DIM = 128
ATT_Q_W = ATT_Q_HEADS * ATT_HEAD_DIM
ATT_KV_W = ATT_KV_HEADS * ATT_HEAD_DIM
ATT_IN_W = ATT_Q_W + 2 * ATT_KV_W
ATT_BLOCK = 128
ROPE_THETA = 10000.0

PEER_HEADS = 8
PEER_N_KEYS = 128
PEER_N_EXPERTS = PEER_N_KEYS * PEER_N_KEYS
PEER_QUERY_DIM = 256
PEER_HALF = PEER_QUERY_DIM // 2
PEER_TOPK = 16
PEER_TOKEN_BLOCK = 128

N_DN_LAYERS = (DEPTH + 1) // 2
N_ATT_LAYERS = DEPTH // 2

kernel_name = 'hybrid_deltanet_gqa_peer_dit'


def rmsnorm(x, gain):
    xf = x.astype(jnp.float32)
    y = xf * lax.rsqrt(jnp.mean(xf * xf, axis=-1, keepdims=True) + EPS)
    return (y * gain.astype(jnp.float32)).astype(x.dtype)


def l2norm(x):
    xf = x.astype(jnp.float32)
    return xf * lax.rsqrt(jnp.sum(xf * xf, axis=-1, keepdims=True) + EPS)


def ada_params(cond, w, b):
    return jnp.split((cond @ w + b)[:, None, :], N_MOD, axis=-1)


def axial_rope_tables(n_tokens):
    rows = n_tokens // GRID_W
    row = jnp.broadcast_to(jnp.arange(rows)[:, None], (rows, GRID_W)).reshape(-1).astype(jnp.float32)
    col = jnp.broadcast_to(jnp.arange(GRID_W)[None, :], (rows, GRID_W)).reshape(-1).astype(jnp.float32)
    axis_dim = ATT_HEAD_DIM // 2
    freqs = ROPE_THETA ** (-jnp.arange(0, axis_dim, 2, dtype=jnp.float32) / axis_dim)
    ang = jnp.concatenate([row[:, None] * freqs, col[:, None] * freqs], axis=-1)
    return jnp.cos(ang), jnp.sin(ang)


def apply_rope(x, cos, sin):
    xf = x.astype(jnp.float32).reshape(x.shape[:-1] + (x.shape[-1] // 2, 2))
    x1, x2 = xf[..., 0], xf[..., 1]
    c = cos[None, :, None, :]
    s = sin[None, :, None, :]
    return jnp.stack([x1 * c - x2 * s, x1 * s + x2 * c], axis=-1).reshape(x.shape).astype(x.dtype)


def centred_dwconv(x, w):
    p = w.shape[0] // 2
    return lax.conv_general_dilated(x, w[:, None, :].astype(x.dtype), window_strides=(1,), padding=[(p, p)],
                                    dimension_numbers=('NWC', 'WIO', 'NWC'), feature_group_count=x.shape[-1])


def gated_delta_chunked(q, k, v, g, beta, s0):
    B, T, H, _ = q.shape
    dv = v.shape[-1]
    n = T // DN_CHUNK

    def chunkify(a):
        a = a.reshape((B, n, DN_CHUNK, H) + a.shape[3:])
        return jnp.moveaxis(a, (1, 3), (0, 2))

    incl = jnp.tril(jnp.ones((DN_CHUNK, DN_CHUNK), dtype=bool))
    strict = jnp.tril(jnp.ones((DN_CHUNK, DN_CHUNK), dtype=bool), -1)
    eye = jnp.eye(DN_CHUNK, dtype=jnp.float32)

    def step(S, inp):
        qc, kc, vc, gc, bc = inp
        gc = jnp.cumsum(gc, axis=-1)
        decay = jnp.exp(jnp.where(incl, gc[..., :, None] - gc[..., None, :], -jnp.inf))
        kk = jnp.einsum('bhid,bhjd->bhij', kc, kc)
        a = eye + jnp.where(strict, kk * decay * bc[..., :, None], 0.0)
        rhs = jnp.concatenate([vc * bc[..., None], kc * (bc * jnp.exp(gc))[..., None]], axis=-1)
        sol = lax.linalg.triangular_solve(a, rhs, left_side=True, lower=True, unit_diagonal=True)
        u, w = sol[..., :dv], sol[..., dv:]
        v_new = u - jnp.einsum('bhcd,bhde->bhce', w, S)
        qk = jnp.where(incl, jnp.einsum('bhid,bhjd->bhij', qc, kc) * decay, 0.0)
        o = (jnp.einsum('bhcd,bhde->bhce', qc * jnp.exp(gc)[..., None], S)
             + jnp.einsum('bhij,bhje->bhie', qk, v_new))
        g_last = gc[..., -1:]
        S = (S * jnp.exp(g_last)[..., None]
             + jnp.einsum('bhcd,bhce->bhde', kc * jnp.exp(g_last - gc)[..., None], v_new))
        return S, o

    S, o = lax.scan(step, s0, (chunkify(q), chunkify(k), chunkify(v), chunkify(g), chunkify(beta)))
    return jnp.moveaxis(o, (0, 2), (1, 3)).reshape(B, T, H, dv), S


def deltanet_mixer(h_ctx, h_lat, w_in, conv_w, a_log, dt_bias, norm_g, w_out, need_ctx_out):
    def prep(h):
        B, T, _ = h.shape
        proj = h @ w_in
        qkv = jax.nn.silu(centred_dwconv(proj[..., :DN_CONV_W], conv_w))
        q = l2norm(qkv[..., :DN_KEY_W].reshape(B, T, DN_QK_HEADS, DN_HEAD_DIM))
        k = l2norm(qkv[..., DN_KEY_W:2 * DN_KEY_W].reshape(B, T, DN_QK_HEADS, DN_HEAD_DIM))
        v = qkv[..., 2 * DN_KEY_W:].reshape(B, T, DN_V_HEADS, DN_HEAD_DIM).astype(jnp.float32)
        q = jnp.repeat(q, DN_REP, axis=2) * (DN_HEAD_DIM ** -0.5)
        k = jnp.repeat(k, DN_REP, axis=2)
        z = proj[..., DN_CONV_W:DN_CONV_W + DN_VAL_W].reshape(B, T, DN_V_HEADS, DN_HEAD_DIM)
        gates = proj[..., DN_CONV_W + DN_VAL_W:].astype(jnp.float32).reshape(B, T, 2, 2, DN_V_HEADS)
        beta = jax.nn.sigmoid(gates[..., 0, :])
        g = -jnp.exp(a_log) * jax.nn.softplus(gates[..., 1, :] + dt_bias)
        return q, k, v, g, beta, z

    def bidir(q, k, v, g, beta, s_f, s_b):
        o_f, s_f = gated_delta_chunked(q, k, v, g[:, :, 0], beta[:, :, 0], s_f)
        fl = lambda a: jnp.flip(a, axis=1)
        o_b, s_b = gated_delta_chunked(fl(q), fl(k), fl(v), fl(g[:, :, 1]), fl(beta[:, :, 1]), s_b)
        return o_f + fl(o_b), s_f, s_b

    def out(o, z, dtype):
        B, T = o.shape[:2]
        o = rmsnorm(o, norm_g) * jax.nn.silu(z.astype(jnp.float32))
        return o.reshape(B, T, DN_VAL_W).astype(dtype) @ w_out

    qc, kc, vc, gc, bc, zc = prep(h_ctx)
    B = h_ctx.shape[0]
    s0 = jnp.zeros((B, DN_V_HEADS, DN_HEAD_DIM, DN_HEAD_DIM), jnp.float32)
    o_c, s_f, s_b = bidir(qc, kc, vc, gc, bc, s0, s0)
    ql, kl, vl, gl, bl, zl = prep(h_lat)
    o_l, _, _ = bidir(ql, kl, vl, gl, bl, s_f, s_b)
    y_c = out(o_c, zc, h_ctx.dtype) if need_ctx_out else None
    return y_c, out(o_l, zl, h_lat.dtype)


def gqa_attend(q, k, v):
    s = jnp.einsum('bqhgd,bkhd->bhgqk', q, k).astype(jnp.float32) * (ATT_HEAD_DIM ** -0.5)
    p = jax.nn.softmax(s, axis=-1).astype(v.dtype)
    return jnp.einsum('bhgqk,bkhd->bqhgd', p, v)


def attention_mixer(h_ctx, h_lat, w_in, qn_g, kn_g, w_out, cos, sin, need_ctx_out):
    def proj(h):
        B, T, _ = h.shape
        p = h @ w_in
        q = p[..., :ATT_Q_W].reshape(B, T, ATT_Q_HEADS, ATT_HEAD_DIM)
        k = rmsnorm(p[..., ATT_Q_W:ATT_Q_W + ATT_KV_W].reshape(B, T, ATT_KV_HEADS, ATT_HEAD_DIM), kn_g)
        v = p[..., ATT_Q_W + ATT_KV_W:].reshape(B, T, ATT_KV_HEADS, ATT_HEAD_DIM)
        return q, k, v

    q_c, k_c, v_c = proj(h_ctx)
    q_l, k_l, v_l = proj(h_lat)
    q_l = apply_rope(rmsnorm(q_l, qn_g), cos, sin)
    k_l = apply_rope(k_l, cos, sin)
    keys = jnp.concatenate([k_c, k_l], axis=1)
    vals = jnp.concatenate([v_c, v_l], axis=1)
    B, T = h_lat.shape[:2]
    nblk = T // ATT_BLOCK
    qb = q_l.reshape(B, nblk, ATT_BLOCK, ATT_KV_HEADS, ATT_GROUP, ATT_HEAD_DIM).swapaxes(0, 1)
    o_l = lax.map(lambda blk: gqa_attend(blk, keys, vals), qb)
    y_l = o_l.swapaxes(0, 1).reshape(B, T, ATT_Q_W) @ w_out
    y_c = None
    if need_ctx_out:
        Lc = h_ctx.shape[1]
        q_c = rmsnorm(q_c, qn_g).reshape(B, Lc, ATT_KV_HEADS, ATT_GROUP, ATT_HEAD_DIM)
        y_c = gqa_attend(q_c, k_c, v_c).reshape(B, Lc, ATT_Q_W) @ w_out
    return y_c, y_l


def peer(h, w_query, sub_keys, u_tab, v_tab):
    B, T, D = h.shape
    blocks = h.reshape(-1, PEER_TOKEN_BLOCK, D)

    def block_fn(xb):
        P = xb.shape[0]
        qry = (xb @ w_query).reshape(P, PEER_HEADS, 2, PEER_HALF)
        s = jnp.einsum('phcd,hckd->phck', qry, sub_keys).astype(jnp.float32)
        top_s, top_i = lax.top_k(s, PEER_TOPK)
        cand_s = (top_s[..., 0, :, None] + top_s[..., 1, None, :]).reshape(P, PEER_HEADS, PEER_TOPK * PEER_TOPK)
        cand_i = (top_i[..., 0, :, None] * PEER_N_KEYS + top_i[..., 1, None, :]).reshape(P, PEER_HEADS, PEER_TOPK * PEER_TOPK)
        best_s, pos = lax.top_k(cand_s, PEER_TOPK)
        idx = jnp.take_along_axis(cand_i, pos, axis=-1)
        gate = jax.nn.softmax(best_s, axis=-1)
        act = jax.nn.gelu(jnp.einsum('pd,phkd->phk', xb, u_tab[idx]).astype(jnp.float32), approximate=False)
        coef = (gate * act).astype(xb.dtype)
        return jnp.einsum('phk,phkd->pd', coef, v_tab[idx])

    return lax.map(block_fn, blocks).reshape(B, T, D)


def setup_inputs(seed: int = 0) -> dict:
    key = jax.random.key(seed)
    ks = jax.random.split(key, 24)
    f32 = jnp.float32

    def nrm(k, shape, scale):
        return jax.random.normal(k, shape, f32) * scale

    def gain(k, shape):
        return 1.0 + 0.02 * jax.random.normal(k, shape, f32)

    dt = jnp.exp(jax.random.uniform(ks[13], (N_DN_LAYERS, 2, DN_V_HEADS), f32, math.log(1e-3), math.log(1e-1)))
    return {
        'x': nrm(ks[0], (BATCH, SEQ, D_MODEL), 1.0),
        'c': nrm(ks[1], (BATCH, D_MODEL), 1.0),
        'ctx': nrm(ks[2], (BATCH, CTX_LEN, D_MODEL), 1.0),
        'c_ctx': nrm(ks[3], (D_MODEL,), 1.0),
        'ada_w': nrm(ks[4], (DEPTH, D_MODEL, N_MOD * D_MODEL), 0.5 * D_MODEL ** -0.5),
        'ada_b': nrm(ks[5], (DEPTH, N_MOD * D_MODEL), 0.02),
        'norm1_g': gain(ks[6], (DEPTH, D_MODEL)),
        'norm2_g': gain(ks[7], (DEPTH, D_MODEL)),
        'final_g': gain(ks[8], (D_MODEL,)),
        'dn_w_in': nrm(ks[9], (N_DN_LAYERS, D_MODEL, DN_IN_W), D_MODEL ** -0.5),
        'dn_conv_w': nrm(ks[10], (N_DN_LAYERS, DN_CONV, DN_CONV_W), DN_CONV ** -0.5),
        'dn_a_log': jnp.log(jax.random.uniform(ks[11], (N_DN_LAYERS, 2, DN_V_HEADS), f32, 1.0, 16.0)),
        'dn_dt_bias': dt + jnp.log(-jnp.expm1(-dt)),
        'dn_norm_g': gain(ks[12], (N_DN_LAYERS, DN_HEAD_DIM)),
        'dn_w_out': nrm(ks[14], (N_DN_LAYERS, DN_VAL_W, D_MODEL), DN_VAL_W ** -0.5),
        'att_w_in': nrm(ks[15], (N_ATT_LAYERS, D_MODEL, ATT_IN_W), D_MODEL ** -0.5),
        'att_qn_g': gain(ks[16], (N_ATT_LAYERS, ATT_HEAD_DIM)),
        'att_kn_g': gain(ks[17], (N_ATT_LAYERS, ATT_HEAD_DIM)),
        'att_w_out': nrm(ks[18], (N_ATT_LAYERS, ATT_Q_W, D_MODEL), ATT_Q_W ** -0.5),
        'peer_w_query': nrm(ks[19], (DEPTH, D_MODEL, PEER_HEADS * PEER_QUERY_DIM), D_MODEL ** -0.5),
        'peer_sub_keys': nrm(ks[20], (DEPTH, PEER_HEADS, 2, PEER_N_KEYS, PEER_HALF), PEER_HALF ** -0.5),
        'peer_u': nrm(ks[21], (DEPTH, PEER_N_EXPERTS, D_MODEL), D_MODEL ** -0.5),
        'peer_v': nrm(ks[22], (DEPTH, PEER_N_EXPERTS, D_MODEL), PEER_HEADS ** -0.5),
    }


def reference(x, c, ctx, c_ctx, ada_w, ada_b, norm1_g, norm2_g, final_g,
              dn_w_in, dn_conv_w, dn_a_log, dn_dt_bias, dn_norm_g, dn_w_out,
              att_w_in, att_qn_g, att_kn_g, att_w_out,
              peer_w_query, peer_sub_keys, peer_u, peer_v):
    cos, sin = axial_rope_tables(x.shape[1])
    cond_lat = jax.nn.silu(c)
    cond_ctx = jax.nn.silu(c_ctx)[None]
    xl, xc = x, ctx
    for i in range(DEPTH):
        last = i == DEPTH - 1
        sh1l, sc1l, g1l, sh2l, sc2l, g2l = ada_params(cond_lat, ada_w[i], ada_b[i])
        sh1c, sc1c, g1c, sh2c, sc2c, g2c = ada_params(cond_ctx, ada_w[i], ada_b[i])
        hl = rmsnorm(xl, norm1_g[i]) * (1.0 + sc1l) + sh1l
        hc = rmsnorm(xc, norm1_g[i]) * (1.0 + sc1c) + sh1c
        j = i // 2
        if i % 2 == 0:
            yc, yl = deltanet_mixer(hc, hl, dn_w_in[j], dn_conv_w[j], dn_a_log[j], dn_dt_bias[j],
                                    dn_norm_g[j], dn_w_out[j], not last)
        else:
            yc, yl = attention_mixer(hc, hl, att_w_in[j], att_qn_g[j], att_kn_g[j], att_w_out[j],
                                     cos, sin, not last)
        xl = xl + g1l * yl
        hl = rmsnorm(xl, norm2_g[i]) * (1.0 + sc2l) + sh2l
        xl = xl + g2l * peer(hl, peer_w_query[i], peer_sub_keys[i], peer_u[i], peer_v[i])
        if not last:
            xc = xc + g1c * yc
            hc = rmsnorm(xc, norm2_g[i]) * (1.0 + sc2c) + sh2c
            xc = xc + g2c * peer(hc, peer_w_query[i], peer_sub_keys[i], peer_u[i], peer_v[i])
    return rmsnorm(xl, final_g)
```

```python
import functools
import math

import jax
import jax.numpy as jnp
from jax import lax
from jax.experimental import pallas as pl
from jax.experimental.pallas import tpu as pltpu

F32 = jnp.float32
BF16 = jnp.bfloat16

EPS = 1e-6
N_MOD = 6
MOD_ROWS = 24
LANES = 128
SUBLANES = 8

HEAD_DIM = 128
DN_QK_HEADS = 8
DN_V_HEADS = 16
DN_CONV = 5
DN_CHUNK = 128
ATT_Q_HEADS = 8
ATT_KV_HEADS = 2
ATT_GROUP = ATT_Q_HEADS // ATT_KV_HEADS
GRID_W = 64
ROPE_THETA = 10000.0
PEER_HEADS = 8
PEER_KEYS = 128
PEER_TOPK = 16

VMEM_LIMIT = 56 * 1024 * 1024


def _cparams(sem):
    return pltpu.CompilerParams(dimension_semantics=sem, vmem_limit_bytes=VMEM_LIMIT)


def _bdot(a, b):
    return jnp.dot(a.astype(BF16), b.astype(BF16), preferred_element_type=F32)


def _bdot_nt(a, b):
    return lax.dot_general(a.astype(BF16), b.astype(BF16), (((1,), (1,)), ((), ())),
                           preferred_element_type=F32)


def _bdot_tn(a, b):
    return lax.dot_general(a.astype(BF16), b.astype(BF16), (((0,), (0,)), ((), ())),
                           preferred_element_type=F32)


def _split3(x):
    hi = x.astype(BF16)
    r = x - hi.astype(F32)
    mid = r.astype(BF16)
    lo = (r - mid.astype(F32)).astype(BF16)
    return hi, mid, lo


def _exact_dot_01(a01, x):
    a = a01.astype(BF16)
    hi, mid, lo = _split3(x)
    return (jnp.dot(a, hi, preferred_element_type=F32) + jnp.dot(a, mid, preferred_element_type=F32)
            + jnp.dot(a, lo, preferred_element_type=F32))


def _mod_row(nlat_blk, blk_per_batch, nbatch):
    def f(i):
        return jnp.where(i < nlat_blk, i // blk_per_batch, nbatch)
    return f


def _ada_kernel(c_ref, w_ref, b_ref, o_ref):
    c = c_ref[...]
    s = c * jax.nn.sigmoid(c)
    o_ref[0] = _bdot(s, w_ref[0]) + b_ref[0]


def ada_all(cond_rows, ada_w, ada_b):
    depth, d, n = ada_w.shape
    tn = 1536
    return pl.pallas_call(
        _ada_kernel,
        out_shape=jax.ShapeDtypeStruct((depth, MOD_ROWS, n), F32),
        grid=(depth, n // tn),
        in_specs=[pl.BlockSpec((MOD_ROWS, d), lambda l, j: (0, 0)),
                  pl.BlockSpec((1, d, tn), lambda l, j: (l, 0, j)),
                  pl.BlockSpec((1, 1, tn), lambda l, j: (l, 0, j))],
        out_specs=pl.BlockSpec((1, MOD_ROWS, tn), lambda l, j: (l, 0, j)),
        compiler_params=_cparams(("parallel", "parallel")),
        name="ada",
    )(cond_rows, ada_w, ada_b.reshape(depth, 1, n))


def _nmm_kernel(x_ref, g_ref, sc_ref, sh_ref, w_ref, o_ref, *rest, with_h):
    if with_h:
        h_ref, hs_ref = rest
    else:
        (hs_ref,) = rest

    @pl.when(pl.program_id(1) == 0)
    def _():
        x = x_ref[...]
        y = x * lax.rsqrt(jnp.mean(x * x, axis=-1, keepdims=True) + EPS) * g_ref[...]
        h = (y * (1.0 + sc_ref[0]) + sh_ref[0]).astype(BF16)
        hs_ref[...] = h
        if with_h:
            h_ref[...] = h

    o_ref[...] = jnp.dot(hs_ref[...], w_ref[...], preferred_element_type=F32)


def norm_mod_matmul(x, gain, scale, shift, w_bf16, *, tm, tn, row_of_block, n_rows=None, with_h=False):
    n, d = x.shape
    n_rows = n if n_rows is None else n_rows
    nout = w_bf16.shape[1]
    out_shape = [jax.ShapeDtypeStruct((n, nout), F32)]
    out_specs = [pl.BlockSpec((tm, tn), lambda i, j: (i, j))]
    if with_h:
        out_shape.append(jax.ShapeDtypeStruct((n, d), BF16))
        out_specs.append(pl.BlockSpec((tm, d), lambda i, j: (i, 0)))
    res = pl.pallas_call(
        functools.partial(_nmm_kernel, with_h=with_h),
        out_shape=out_shape,
        grid=(n_rows // tm, nout // tn),
        in_specs=[pl.BlockSpec((tm, d), lambda i, j: (i, 0)),
                  pl.BlockSpec((1, d), lambda i, j: (0, 0)),
                  pl.BlockSpec((1, 1, d), lambda i, j: (row_of_block(i), 0, 0)),
                  pl.BlockSpec((1, 1, d), lambda i, j: (row_of_block(i), 0, 0)),
                  pl.BlockSpec((d, tn), lambda i, j: (0, j))],
        out_specs=out_specs,
        scratch_shapes=[pltpu.VMEM((tm, d), BF16)],
        compiler_params=_cparams(("parallel", "arbitrary")),
        name="norm_mod_matmul",
    )(x, gain.reshape(1, d), scale, shift, w_bf16)
    return res if with_h else res[0]


def _mm_res_kernel(a_ref, w_ref, x_ref, gate_ref, o_ref):
    y = jnp.dot(a_ref[...].astype(BF16), w_ref[...], preferred_element_type=F32)
    o_ref[...] = x_ref[...] + gate_ref[0] * y


def matmul_residual(a, w_bf16, x, gate, *, tm, row_of_block, n_rows):
    n, d = x.shape
    k = a.shape[1]
    return pl.pallas_call(
        _mm_res_kernel,
        out_shape=jax.ShapeDtypeStruct((n, d), F32),
        grid=(n_rows // tm,),
        in_specs=[pl.BlockSpec((tm, k), lambda i: (i, 0)),
                  pl.BlockSpec((k, d), lambda i: (0, 0)),
                  pl.BlockSpec((tm, d), lambda i: (i, 0)),
                  pl.BlockSpec((1, 1, d), lambda i: (row_of_block(i), 0, 0))],
        out_specs=pl.BlockSpec((tm, d), lambda i: (i, 0)),
        compiler_params=_cparams(("parallel",)),
        name="matmul_residual",
    )(a, w_bf16, x, gate)


def _dn_prep_kernel(x_ref, xp_ref, xn_ref, w_ref, o_ref, *, tm, nlat_blk, blk_per_lat, blk_per_ctx):
    i = pl.program_id(0)
    j = pl.program_id(1)
    is_lat = i < nlat_blk
    pos = jnp.where(is_lat, i % blk_per_lat, (i - nlat_blk) % blk_per_ctx)
    nblk = jnp.where(is_lat, blk_per_lat, blk_per_ctx)
    x = x_ref[...]
    xp = jnp.where(pos > 0, xp_ref[...], 0.0)
    xn = jnp.where(pos < nblk - 1, xn_ref[...], 0.0)
    xe = jnp.concatenate([xp, x, xn], axis=0)
    half = DN_CONV // 2
    y = None
    for t in range(DN_CONV):
        lo = SUBLANES - half + t
        term = w_ref[t:t + 1, :] * xe[lo:lo + tm, :]
        y = term if y is None else y + term
    y = y * jax.nn.sigmoid(y)

    @pl.when(j < 2)
    def _():
        post = jnp.where(j == 0, HEAD_DIM ** -0.5, 1.0)
        for hh in range(y.shape[1] // HEAD_DIM):
            seg = y[:, hh * HEAD_DIM:(hh + 1) * HEAD_DIM]
            inv = lax.rsqrt(jnp.sum(seg * seg, axis=-1, keepdims=True) + EPS)
            o_ref[:, hh * HEAD_DIM:(hh + 1) * HEAD_DIM] = seg * inv * post

    @pl.when(j >= 2)
    def _():
        o_ref[...] = y


def dn_prep(proj, conv_w, *, tm, nlat_blk, blk_per_lat, blk_per_ctx):
    n = proj.shape[0]
    tc = DN_QK_HEADS * HEAD_DIM
    ncol = conv_w.shape[1] // tc
    sub = tm // SUBLANES
    last = n // SUBLANES - 1
    return pl.pallas_call(
        functools.partial(_dn_prep_kernel, tm=tm, nlat_blk=nlat_blk, blk_per_lat=blk_per_lat,
                          blk_per_ctx=blk_per_ctx),
        out_shape=jax.ShapeDtypeStruct((n, conv_w.shape[1]), F32),
        grid=(n // tm, ncol),
        in_specs=[pl.BlockSpec((tm, tc), lambda i, j: (i, j)),
                  pl.BlockSpec((SUBLANES, tc), lambda i, j: (jnp.maximum(i * sub - 1, 0), j)),
                  pl.BlockSpec((SUBLANES, tc), lambda i, j: (jnp.minimum((i + 1) * sub, last), j)),
                  pl.BlockSpec((DN_CONV, tc), lambda i, j: (0, j))],
        out_specs=pl.BlockSpec((tm, tc), lambda i, j: (i, j)),
        compiler_params=_cparams(("parallel", "parallel")),
        name="dn_prep",
    )(proj, proj, proj, conv_w)


def _dn_gates_kernel(x_ref, alog_ref, dtb_ref, perm_ref, rows_ref, cols_ref, *, tm):
    x = x_ref[...]
    lane = lax.broadcasted_iota(jnp.int32, x.shape, 1)
    is_a = (lane % 32) >= DN_V_HEADS
    is_bwd = lane >= 32
    beta = jax.nn.sigmoid(x)
    g = -jnp.exp(alog_ref[...]) * jax.nn.softplus(x + dtb_ref[...])
    c = DN_CHUNK
    ii = lax.broadcasted_iota(jnp.int32, (c, c), 0)
    jj = lax.broadcasted_iota(jnp.int32, (c, c), 1)
    lower = (ii >= jj).astype(F32)
    upper = (ii <= jj).astype(F32)
    chunk_is_bwd = lax.broadcasted_iota(jnp.int32, (c, x.shape[1]), 1) >= 32
    parts = []
    for ch in range(tm // c):
        gch = g[ch * c:(ch + 1) * c, :]
        parts.append(jnp.where(chunk_is_bwd, _exact_dot_01(upper, gch), _exact_dot_01(lower, gch)))
    gcum = jnp.concatenate(parts, axis=0) if len(parts) > 1 else parts[0]
    vals = jnp.where(is_a, gcum, beta)
    hi, mid, lo = _split3(vals)
    p = perm_ref[...]
    grouped = (jnp.dot(hi, p, preferred_element_type=F32) + jnp.dot(mid, p, preferred_element_type=F32)
               + jnp.dot(lo, p, preferred_element_type=F32))
    gt = grouped.T
    for hq in range(DN_QK_HEADS):
        cols_ref[hq] = grouped[:, hq * SUBLANES:(hq + 1) * SUBLANES]
        rows_ref[hq] = gt[hq * SUBLANES:(hq + 1) * SUBLANES, :]


def dn_gates(proj, a_log, dt_bias, *, tm, gate_col_block):
    n = proj.shape[0]
    zeros16 = jnp.zeros((2, DN_V_HEADS), F32)
    alog = jnp.pad(jnp.stack([zeros16, a_log], axis=1).reshape(1, 64), ((0, 0), (0, 64)))
    dtb = jnp.pad(jnp.stack([zeros16, dt_bias], axis=1).reshape(1, 64), ((0, 0), (0, 64)))
    src = jnp.arange(64)
    d_, kind, hv = src // 32, (src % 32) // 16, src % 16
    dst = (hv // 2) * 8 + d_ * 4 + kind * 2 + (hv % 2)
    perm = jnp.zeros((LANES, LANES), BF16).at[src, dst].set(1.0)
    return pl.pallas_call(
        functools.partial(_dn_gates_kernel, tm=tm),
        out_shape=[jax.ShapeDtypeStruct((DN_QK_HEADS, SUBLANES, n), F32),
                   jax.ShapeDtypeStruct((DN_QK_HEADS, n, SUBLANES), F32)],
        grid=(n // tm,),
        in_specs=[pl.BlockSpec((tm, LANES), lambda i: (i, gate_col_block)),
                  pl.BlockSpec((1, LANES), lambda i: (0, 0)),
                  pl.BlockSpec((1, LANES), lambda i: (0, 0)),
                  pl.BlockSpec((LANES, LANES), lambda i: (0, 0))],
        out_specs=[pl.BlockSpec((DN_QK_HEADS, SUBLANES, tm), lambda i: (0, 0, i)),
                   pl.BlockSpec((DN_QK_HEADS, tm, SUBLANES), lambda i: (0, i, 0))],
        compiler_params=_cparams(("parallel",)),
        name="dn_gates",
    )(proj, alog, dtb, perm)


def _dot_split(a, b):
    a_hi = a.astype(BF16)
    a_lo = (a - a_hi.astype(F32)).astype(BF16)
    b_hi = b.astype(BF16)
    b_lo = (b - b_hi.astype(F32)).astype(BF16)
    return (jnp.dot(a_hi, b_hi, preferred_element_type=F32) + jnp.dot(a_hi, b_lo, preferred_element_type=F32)
            + jnp.dot(a_lo, b_hi, preferred_element_type=F32))


TRI_BASE = 8


def _tri_inverse_minus_eye(nm, ii, jj):
    c = nm.shape[0]
    xr = ii ^ jj
    k = int(math.log2(TRI_BASE))
    nd = jnp.where((xr >> k) == 0, nm, 0.0)
    x = -nd
    pw = nd
    for _ in range(k - 1):
        pw = _dot_split(pw, pw)
        x = x + pw + _dot_split(x, pw)
    while (1 << k) < c:
        cm = jnp.where((xr >> k) == 1, nm, 0.0)
        t = cm + _bdot(x, cm)
        x = x - (t + _bdot(t, x))
        k += 1
    return x


def _delta_chunk(q, k, v, bcol, gcol, grow, s, lower):
    c = q.shape[0]
    ii = lax.broadcasted_iota(jnp.int32, (c, c), 0)
    jj = lax.broadcasted_iota(jnp.int32, (c, c), 1)
    incl = (ii >= jj) if lower else (ii <= jj)
    strict = (ii > jj) if lower else (ii < jj)
    decay = jnp.exp(jnp.where(incl, gcol - grow, -jnp.inf))
    kk = _bdot_nt(k, k)
    nm = jnp.where(strict, kk * decay * bcol, 0.0)
    x = _tri_inverse_minus_eye(nm, ii, jj)
    egc = jnp.exp(gcol)
    rhs = jnp.concatenate([v * bcol, k * (bcol * egc)], axis=1)
    sol = rhs + _bdot(x, rhs)
    dv = v.shape[1]
    u, w = sol[:, :dv], sol[:, dv:]
    v_new = u - _bdot(w, s)
    qk = jnp.where(incl, _bdot_nt(q, k) * decay, 0.0)
    o = _bdot(q * egc, s) + _bdot(qk, v_new)
    g_last = gcol[c - 1:c, :] if lower else gcol[0:1, :]
    s_new = s * jnp.exp(g_last) + _bdot_tn(k * jnp.exp(g_last - gcol), v_new)
    return o, s_new


def _dn_scan_kernel(qf_ref, kf_ref, vf_ref, rf_ref, cf_ref, qb_ref, kb_ref, vb_ref, rb_ref, cb_ref,
                    of_ref, ob_ref, s_ref):
    @pl.when(pl.program_id(2) == 0)
    def _():
        s_ref[...] = jnp.zeros_like(s_ref)

    rep = DN_V_HEADS // DN_QK_HEADS
    dirs = ((qf_ref, kf_ref, vf_ref, rf_ref, cf_ref, of_ref, True),
            (qb_ref, kb_ref, vb_ref, rb_ref, cb_ref, ob_ref, False))
    for d, (q_ref, k_ref, v_ref, r_ref, c_ref, o_ref, lower) in enumerate(dirs):
        q = q_ref[...]
        k = k_ref[...]
        rows = r_ref[0]
        cols = c_ref[0]
        for r in range(rep):
            ib = d * 4 + r
            ig = d * 4 + 2 + r
            v = v_ref[:, r * HEAD_DIM:(r + 1) * HEAD_DIM]
            o, s_new = _delta_chunk(q, k, v, cols[:, ib:ib + 1], cols[:, ig:ig + 1], rows[ig:ig + 1, :],
                                    s_ref[d * rep + r], lower)
            o_ref[:, r * HEAD_DIM:(r + 1) * HEAD_DIM] = o
            s_ref[d * rep + r] = s_new


def dn_scan(qkv, rows, cols, *, nbatch, t_lat, t_ctx):
    n = qkv.shape[0]
    c = DN_CHUNK
    nc, nl = t_ctx // c, t_lat // c
    lat0 = lambda b: b * nl
    ctx0 = lambda b: (nbatch * t_lat) // c + b * nc

    def fwd(b, s):
        return jnp.where(s < nc, ctx0(b) + s, lat0(b) + (s - nc))

    def bwd(b, s):
        return jnp.where(s < nc, ctx0(b) + (nc - 1 - s), lat0(b) + (nl - 1 - (s - nc)))

    vw = HEAD_DIM * (DN_V_HEADS // DN_QK_HEADS)
    v_col0 = 2 * DN_QK_HEADS * HEAD_DIM // vw

    def specs(blk):
        return [pl.BlockSpec((c, HEAD_DIM), lambda b, h, s: (blk(b, s), h)),
                pl.BlockSpec((c, HEAD_DIM), lambda b, h, s: (blk(b, s), DN_QK_HEADS + h)),
                pl.BlockSpec((c, vw), lambda b, h, s: (blk(b, s), v_col0 + h)),
                pl.BlockSpec((1, SUBLANES, c), lambda b, h, s: (h, 0, blk(b, s))),
                pl.BlockSpec((1, c, SUBLANES), lambda b, h, s: (h, blk(b, s), 0))]

    ow = DN_V_HEADS * HEAD_DIM
    return pl.pallas_call(
        _dn_scan_kernel,
        out_shape=[jax.ShapeDtypeStruct((n, ow), F32), jax.ShapeDtypeStruct((n, ow), F32)],
        grid=(nbatch, DN_QK_HEADS, nc + nl),
        in_specs=specs(fwd) + specs(bwd),
        out_specs=[pl.BlockSpec((c, vw), lambda b, h, s: (fwd(b, s), h)),
                   pl.BlockSpec((c, vw), lambda b, h, s: (bwd(b, s), h))],
        scratch_shapes=[pltpu.VMEM((2 * (DN_V_HEADS // DN_QK_HEADS), HEAD_DIM, HEAD_DIM), F32)],
        compiler_params=_cparams(("parallel", "parallel", "arbitrary")),
        name="dn_scan",
    )(qkv, qkv, qkv, rows, cols, qkv, qkv, qkv, rows, cols)


def _dn_out_kernel(of_ref, ob_ref, z_ref, ng_ref, w_ref, x_ref, gate_ref, o_ref):
    o = of_ref[...] + ob_ref[...]
    z = z_ref[...]
    parts = []
    for hh in range(DN_V_HEADS):
        seg = o[:, hh * HEAD_DIM:(hh + 1) * HEAD_DIM]
        y = seg * lax.rsqrt(jnp.mean(seg * seg, axis=-1, keepdims=True) + EPS) * ng_ref[...]
        zz = z[:, hh * HEAD_DIM:(hh + 1) * HEAD_DIM]
        parts.append((y * (zz * jax.nn.sigmoid(zz))).astype(BF16))
    a = jnp.concatenate(parts, axis=1)
    o_ref[...] = x_ref[...] + gate_ref[0] * jnp.dot(a, w_ref[...], preferred_element_type=F32)


def dn_out(o_f, o_b, proj, norm_g, w_bf16, x, gate, *, tm, row_of_block, z_col_block):
    n, d = x.shape
    k = o_f.shape[1]
    return pl.pallas_call(
        _dn_out_kernel,
        out_shape=jax.ShapeDtypeStruct((n, d), F32),
        grid=(n // tm,),
        in_specs=[pl.BlockSpec((tm, k), lambda i: (i, 0)),
                  pl.BlockSpec((tm, k), lambda i: (i, 0)),
                  pl.BlockSpec((tm, k), lambda i: (i, z_col_block)),
                  pl.BlockSpec((1, HEAD_DIM), lambda i: (0, 0)),
                  pl.BlockSpec((k, d), lambda i: (0, 0)),
                  pl.BlockSpec((tm, d), lambda i: (i, 0)),
                  pl.BlockSpec((1, 1, d), lambda i: (row_of_block(i), 0, 0))],
        out_specs=pl.BlockSpec((tm, d), lambda i: (i, 0)),
        compiler_params=_cparams(("parallel",)),
        name="dn_out",
    )(o_f, o_b, proj, norm_g.reshape(1, HEAD_DIM), w_bf16, x, gate)


def _swap_pairs(x):
    lane = lax.broadcasted_iota(jnp.int32, x.shape, 1)
    nxt = pltpu.roll(x, x.shape[1] - 1, 1)
    prv = pltpu.roll(x, 1, 1)
    return jnp.where(lane % 2 == 0, nxt, prv)


def _att_prep_kernel(p_ref, qg_ref, kg_ref, cos_ref, sin_ref, q_ref, k_ref, v_ref):
    cos = cos_ref[...]
    sin = sin_ref[...]

    def norm_rope(seg, gain):
        y = seg * lax.rsqrt(jnp.mean(seg * seg, axis=-1, keepdims=True) + EPS) * gain
        return (y * cos + _swap_pairs(y) * sin).astype(BF16)

    qw = ATT_Q_HEADS * HEAD_DIM
    for hh in range(ATT_Q_HEADS):
        q_ref[:, hh * HEAD_DIM:(hh + 1) * HEAD_DIM] = norm_rope(
            p_ref[:, hh * HEAD_DIM:(hh + 1) * HEAD_DIM], qg_ref[...])
    for hh in range(ATT_KV_HEADS):
        k_ref[:, hh * HEAD_DIM:(hh + 1) * HEAD_DIM] = norm_rope(
            p_ref[:, qw + hh * HEAD_DIM:qw + (hh + 1) * HEAD_DIM], kg_ref[...])
    kvw = ATT_KV_HEADS * HEAD_DIM
    v_ref[...] = p_ref[:, qw + kvw:qw + 2 * kvw].astype(BF16)


def att_prep(proj, qn_g, kn_g, cos2, sin2, *, tm, nlat_blk, blk_per_lat):
    n, pw = proj.shape
    qw = ATT_Q_HEADS * HEAD_DIM
    kvw = ATT_KV_HEADS * HEAD_DIM
    ident_blk = blk_per_lat
    tab = lambda i: (jnp.where(i < nlat_blk, i % blk_per_lat, ident_blk), 0)
    return pl.pallas_call(
        _att_prep_kernel,
        out_shape=[jax.ShapeDtypeStruct((n, qw), BF16), jax.ShapeDtypeStruct((n, kvw), BF16),
                   jax.ShapeDtypeStruct((n, kvw), BF16)],
        grid=(n // tm,),
        in_specs=[pl.BlockSpec((tm, pw), lambda i: (i, 0)),
                  pl.BlockSpec((1, HEAD_DIM), lambda i: (0, 0)),
                  pl.BlockSpec((1, HEAD_DIM), lambda i: (0, 0)),
                  pl.BlockSpec((tm, HEAD_DIM), tab),
                  pl.BlockSpec((tm, HEAD_DIM), tab)],
        out_specs=[pl.BlockSpec((tm, qw), lambda i: (i, 0)),
                   pl.BlockSpec((tm, kvw), lambda i: (i, 0)),
                   pl.BlockSpec((tm, kvw), lambda i: (i, 0))],
        compiler_params=_cparams(("parallel",)),
        name="att_prep",
    )(proj, qn_g.reshape(1, HEAD_DIM), kn_g.reshape(1, HEAD_DIM), cos2, sin2)


def rope_tables(t_lat, tm):
    rows = t_lat // GRID_W
    row = jnp.broadcast_to(jnp.arange(rows)[:, None], (rows, GRID_W)).reshape(-1).astype(F32)
    col = jnp.broadcast_to(jnp.arange(GRID_W)[None, :], (rows, GRID_W)).reshape(-1).astype(F32)
    axis_dim = HEAD_DIM // 2
    freqs = ROPE_THETA ** (-jnp.arange(0, axis_dim, 2, dtype=F32) / axis_dim)
    ang = jnp.concatenate([row[:, None] * freqs, col[:, None] * freqs], axis=-1)
    cos, sin = jnp.cos(ang), jnp.sin(ang)
    cos2 = jnp.repeat(cos, 2, axis=-1)
    sin2 = jnp.stack([-sin, sin], axis=-1).reshape(t_lat, HEAD_DIM)
    cos2 = jnp.concatenate([cos2, jnp.ones((tm, HEAD_DIM), F32)], axis=0)
    sin2 = jnp.concatenate([sin2, jnp.zeros((tm, HEAD_DIM), F32)], axis=0)
    return cos2, sin2


def _attn_kernel(q_ref, kc_ref, vc_ref, *rest, with_lat, kchunk):
    if with_lat:
        kl_ref, vl_ref, o_ref = rest
    else:
        _, o_ref = rest
    tq = q_ref.shape[0]
    q = jnp.concatenate([q_ref[:, r * HEAD_DIM:(r + 1) * HEAD_DIM] for r in range(ATT_GROUP)], axis=0)
    scale = HEAD_DIM ** -0.5
    pieces = [(kc_ref, vc_ref, 0, kc_ref.shape[0])]
    if with_lat:
        for c0 in range(0, kl_ref.shape[0], kchunk):
            pieces.append((kl_ref, vl_ref, c0, kchunk))
    m = l = acc = None
    for k_ref, v_ref, c0, cn in pieces:
        s = lax.dot_general(q, k_ref[c0:c0 + cn, :], (((1,), (1,)), ((), ())),
                            preferred_element_type=F32) * scale
        mx = jnp.max(s, axis=-1, keepdims=True)
        if m is None:
            m = mx
            p = jnp.exp(s - m)
            l = jnp.sum(p, axis=-1, keepdims=True)
            acc = jnp.dot(p.astype(BF16), v_ref[c0:c0 + cn, :], preferred_element_type=F32)
        else:
            m_new = jnp.maximum(m, mx)
            alpha = jnp.exp(m - m_new)
            p = jnp.exp(s - m_new)
            l = alpha * l + jnp.sum(p, axis=-1, keepdims=True)
            acc = alpha * acc + jnp.dot(p.astype(BF16), v_ref[c0:c0 + cn, :], preferred_element_type=F32)
            m = m_new
    o = acc / l
    for r in range(ATT_GROUP):
        o_ref[:, r * HEAD_DIM:(r + 1) * HEAD_DIM] = o[r * tq:(r + 1) * tq, :].astype(BF16)


def attention(qn, kn, vn, o_lat=None, *, nbatch, t_lat, t_ctx, tq):
    latent_queries = o_lat is None
    n = qn.shape[0]
    gw = ATT_GROUP * HEAD_DIM
    lat_rows = nbatch * t_lat
    ctx_blk = lambda b: (lat_rows // t_ctx) + b
    if latent_queries:
        qblk = lambda b, qi: b * (t_lat // tq) + qi
        nq = t_lat // tq
    else:
        qblk = lambda b, qi: lat_rows // tq + b * (t_ctx // tq) + qi
        nq = t_ctx // tq
    in_specs = [pl.BlockSpec((tq, gw), lambda b, g, qi: (qblk(b, qi), g)),
                pl.BlockSpec((t_ctx, HEAD_DIM), lambda b, g, qi: (ctx_blk(b), g)),
                pl.BlockSpec((t_ctx, HEAD_DIM), lambda b, g, qi: (ctx_blk(b), g))]
    args = [qn, kn, vn]
    if latent_queries:
        in_specs += [pl.BlockSpec((t_lat, HEAD_DIM), lambda b, g, qi: (b, g)),
                     pl.BlockSpec((t_lat, HEAD_DIM), lambda b, g, qi: (b, g))]
        args += [kn, vn]
        aliases = {}
    else:
        in_specs.append(pl.BlockSpec(memory_space=pl.ANY))
        args.append(o_lat)
        aliases = {len(args) - 1: 0}
    kchunk = min(t_lat, 1024)
    return pl.pallas_call(
        functools.partial(_attn_kernel, with_lat=latent_queries, kchunk=kchunk),
        out_shape=jax.ShapeDtypeStruct((n, ATT_Q_HEADS * HEAD_DIM), BF16),
        grid=(nbatch, ATT_KV_HEADS, nq),
        in_specs=in_specs,
        out_specs=pl.BlockSpec((tq, gw), lambda b, g, qi: (qblk(b, qi), g)),
        input_output_aliases=aliases,
        compiler_params=_cparams(("parallel", "parallel", "arbitrary")),
        name="attention_lat" if latent_queries else "attention_ctx",
    )(*args)


N_CAND = PEER_TOPK + 1


def _top_values(ref, s, n):
    cur = s
    for r in range(n):
        m = jnp.max(cur, axis=0, keepdims=True)
        ref[r:r + 1, :] = m
        cur = jnp.where(cur >= m, -jnp.inf, cur)


def _peer_scores_kernel(q_ref, sk_ref, e2_ref, thr_ref, e1_ref, a_ref, b_ref):
    p = q_ref.shape[0]
    neg = -jnp.inf
    a_ref[...] = jnp.full(a_ref.shape, neg, F32)
    b_ref[...] = jnp.full(b_ref.shape, neg, F32)
    for h in range(PEER_HEADS):
        s1 = _bdot_nt(sk_ref[h, 0], q_ref[:, (2 * h) * PEER_KEYS:(2 * h + 1) * PEER_KEYS])
        s2 = _bdot_nt(sk_ref[h, 1], q_ref[:, (2 * h + 1) * PEER_KEYS:(2 * h + 2) * PEER_KEYS])
        _top_values(a_ref, s1, N_CAND)
        _top_values(b_ref, s2, N_CAND)
        a = a_ref[...]
        b = b_ref[...]
        a0, b0 = a[0:1, :], b[0:1, :]
        cands = [a0 + b, a[1:N_CAND, :] + b0]
        ridx = lax.broadcasted_iota(jnp.int32, (SUBLANES, p), 0)
        for i in range(1, SUBLANES):
            nj = N_CAND // (i + 1) - 1
            if nj < 1:
                break
            cands.append(jnp.where(ridx < nj, a[i:i + 1, :] + b[1:1 + SUBLANES, :], neg))
        cand = jnp.concatenate(cands, axis=0)
        cur = cand
        c_prev = None
        for r in range(N_CAND):
            m = jnp.max(cur, axis=0, keepdims=True)
            if r == N_CAND - 2:
                c_prev = m
            cur = jnp.where(cur >= m, neg, cur)
        tau = 0.5 * (c_prev + m)
        top = a0 + b0
        z = jnp.sum(jnp.where(cand > tau, jnp.exp(cand - top), 0.0), axis=0, keepdims=True)
        e2_ref[0, h] = jnp.exp(s2 - b0)
        thr_ref[0, h] = jnp.exp((tau - b0) - s1)
        e1_ref[0, h] = jnp.exp(s1 - a0) * (0.5 / z)


def peer_scores(qry, sub_keys_bf16, *, tp, n_rows):
    nblk = n_rows // tp
    shp = jax.ShapeDtypeStruct((nblk, PEER_HEADS, PEER_KEYS, tp), F32)
    spec = pl.BlockSpec((1, PEER_HEADS, PEER_KEYS, tp), lambda i: (i, 0, 0, 0))
    return pl.pallas_call(
        _peer_scores_kernel,
        out_shape=[shp, shp, shp],
        grid=(nblk,),
        in_specs=[pl.BlockSpec((tp, qry.shape[1]), lambda i: (i, 0)),
                  pl.BlockSpec(sub_keys_bf16.shape, lambda i: (0, 0, 0, 0))],
        out_specs=[spec, spec, spec],
        scratch_shapes=[pltpu.VMEM((24, tp), F32), pltpu.VMEM((24, tp), F32)],
        compiler_params=_cparams(("parallel",)),
        name="peer_scores",
    )(qry, sub_keys_bf16)


PEER_ROWS_PER_STEP = 8
PEER_SUB = 2


def _peer_expert_kernel(h_ref, u_ref, vt_ref, e2_ref, thr_ref, e1_ref, x_ref, gate_ref, o_ref, acc_ref):
    e = pl.program_id(1)

    @pl.when(e == 0)
    def _():
        acc_ref[...] = jnp.zeros_like(acc_ref)

    tp = h_ref.shape[0]
    h = h_ref[...]
    nsub = PEER_ROWS_PER_STEP // PEER_SUB
    width = PEER_SUB * PEER_KEYS
    inv_sqrt2 = 1.0 / math.sqrt(2.0)
    for sb in range(nsub):
        act = lax.dot_general(u_ref[sb * width:(sb + 1) * width, :], h, (((1,), (1,)), ((), ())),
                              preferred_element_type=F32)
        coefs = []
        for tl in range(tp // LANES):
            ls = slice(tl * LANES, (tl + 1) * LANES)
            for r in range(PEER_SUB):
                row = sb * PEER_SUB + r
                m = None
                for hd in range(PEER_HEADS):
                    e2 = e2_ref[0, hd, :, ls]
                    sel = jnp.where(e2 >= thr_ref[0, hd, row:row + 1, ls], e2, 0.0)
                    term = sel * e1_ref[0, hd, row:row + 1, ls]
                    m = term if m is None else m + term
                a = act[r * PEER_KEYS:(r + 1) * PEER_KEYS, ls]
                coefs.append((r, tl, (m * (a * (1.0 + lax.erf(a * inv_sqrt2)))).astype(BF16)))
        coef = jnp.concatenate(
            [jnp.concatenate([c for (r, tl, c) in coefs if r == rr], axis=1) for rr in range(PEER_SUB)], axis=0)
        acc_ref[...] += jnp.dot(vt_ref[:, sb * width:(sb + 1) * width], coef, preferred_element_type=F32)

    @pl.when(e == pl.num_programs(1) - 1)
    def _():
        o_ref[...] = x_ref[...] + gate_ref[0] * acc_ref[...].T


def peer_experts(h_bf16, u_bf16, vt_bf16, e2, thr, e1, x, gate, *, tp, row_of_block, n_rows):
    n, d = x.shape
    n_exp = u_bf16.shape[0]
    rows = PEER_ROWS_PER_STEP
    ew = rows * PEER_KEYS
    sspec = pl.BlockSpec((1, PEER_HEADS, PEER_KEYS, tp), lambda i, e: (i, 0, 0, 0))
    rspec = pl.BlockSpec((1, PEER_HEADS, rows, tp), lambda i, e: (i, 0, e, 0))
    return pl.pallas_call(
        _peer_expert_kernel,
        out_shape=jax.ShapeDtypeStruct((n, d), F32),
        grid=(n_rows // tp, n_exp // ew),
        in_specs=[pl.BlockSpec((tp, d), lambda i, e: (i, 0)),
                  pl.BlockSpec((ew, d), lambda i, e: (e, 0)),
                  pl.BlockSpec((d, ew), lambda i, e: (0, e)),
                  sspec, rspec, rspec,
                  pl.BlockSpec((tp, d), lambda i, e: (i, 0)),
                  pl.BlockSpec((1, 1, d), lambda i, e: (row_of_block(i), 0, 0))],
        out_specs=pl.BlockSpec((tp, d), lambda i, e: (i, 0)),
        scratch_shapes=[pltpu.VMEM((d, tp), F32)],
        compiler_params=_cparams(("parallel", "arbitrary")),
        name="peer_experts",
    )(h_bf16, u_bf16, vt_bf16, e2, thr, e1, x, gate)


def _final_norm_kernel(x_ref, g_ref, o_ref):
    x = x_ref[...]
    o_ref[...] = x * lax.rsqrt(jnp.mean(x * x, axis=-1, keepdims=True) + EPS) * g_ref[...]


def final_norm(x, gain, *, tm, n_rows):
    d = x.shape[1]
    return pl.pallas_call(
        _final_norm_kernel,
        out_shape=jax.ShapeDtypeStruct((n_rows, d), F32),
        grid=(n_rows // tm,),
        in_specs=[pl.BlockSpec((tm, d), lambda i: (i, 0)), pl.BlockSpec((1, d), lambda i: (0, 0))],
        out_specs=pl.BlockSpec((tm, d), lambda i: (i, 0)),
        compiler_params=_cparams(("parallel",)),
        name="final_norm",
    )(x, gain.reshape(1, d))


def kernel(x, c, ctx, c_ctx, ada_w, ada_b, norm1_g, norm2_g, final_g, dn_w_in, dn_conv_w, dn_a_log, dn_dt_bias,
           dn_norm_g, dn_w_out, att_w_in, att_qn_g, att_kn_g, att_w_out, peer_w_query, peer_sub_keys, peer_u,
           peer_v):
    nbatch, t_lat, d = x.shape
    t_ctx = ctx.shape[1]
    depth = ada_w.shape[0]
    n_lat = nbatch * t_lat
    n_all = n_lat + nbatch * t_ctx
    assert nbatch + 1 <= MOD_ROWS

    tm = min(512, t_ctx * nbatch, t_lat)
    ts = min(256, t_ctx)
    assert t_lat % tm == 0 and (nbatch * t_ctx) % tm == 0 and t_lat % ts == 0 and t_ctx % ts == 0
    assert t_lat % DN_CHUNK == 0 and t_ctx % DN_CHUNK == 0 and ts % DN_CHUNK == 0
    row_tm = _mod_row(n_lat // tm, t_lat // tm, nbatch)

    xs = jnp.concatenate([x.reshape(n_lat, d), ctx.reshape(nbatch * t_ctx, d)], axis=0)
    cond_rows = jnp.concatenate([c, c_ctx[None, :], jnp.zeros((MOD_ROWS - nbatch - 1, d), F32)], axis=0)
    mod = ada_all(cond_rows, ada_w, ada_b).reshape(depth, MOD_ROWS, N_MOD, 1, d)
    cos2, sin2 = rope_tables(t_lat, ts)

    dn_cols = dn_w_in.shape[2]
    dn_pad = (-dn_cols) % (7 * LANES)
    for i in range(depth):
        last = i == depth - 1
        sh1, sc1, g1, sh2, sc2, g2 = (mod[i, :, m] for m in range(N_MOD))
        j = i // 2
        n_act = n_lat if last else n_all
        if i % 2 == 0:
            w_in = jnp.pad(dn_w_in[j], ((0, 0), (0, dn_pad))).astype(BF16)
            proj = norm_mod_matmul(xs, norm1_g[i], sc1, sh1, w_in, tm=tm, tn=7 * LANES, row_of_block=row_tm)
            conv_w = dn_conv_w[j]
            qkv = dn_prep(proj, conv_w, tm=ts, nlat_blk=n_lat // ts, blk_per_lat=t_lat // ts,
                          blk_per_ctx=t_ctx // ts)
            conv_cols = conv_w.shape[1]
            val_w = DN_V_HEADS * HEAD_DIM
            rows, cols = dn_gates(proj, dn_a_log[j], dn_dt_bias[j], tm=ts,
                                  gate_col_block=(conv_cols + val_w) // LANES)
            o_f, o_b = dn_scan(qkv, rows, cols, nbatch=nbatch, t_lat=t_lat, t_ctx=t_ctx)
            xs = dn_out(o_f, o_b, proj, dn_norm_g[j], dn_w_out[j].astype(BF16), xs, g1, tm=tm,
                        row_of_block=row_tm, z_col_block=conv_cols // val_w)
        else:
            w_in = att_w_in[j].astype(BF16)
            proj = norm_mod_matmul(xs, norm1_g[i], sc1, sh1, w_in, tm=tm, tn=w_in.shape[1], row_of_block=row_tm)
            qn, kn, vn = att_prep(proj, att_qn_g[j], att_kn_g[j], cos2, sin2, tm=ts, nlat_blk=n_lat // ts,
                                  blk_per_lat=t_lat // ts)
            tq = min(128, t_ctx)
            o = attention(qn, kn, vn, nbatch=nbatch, t_lat=t_lat, t_ctx=t_ctx, tq=tq)
            if not last:
                o = attention(qn, kn, vn, o, nbatch=nbatch, t_lat=t_lat, t_ctx=t_ctx, tq=tq)
            xs = matmul_residual(o, att_w_out[j].astype(BF16), xs, g1, tm=tm, row_of_block=row_tm, n_rows=n_act)
        qry, h2 = norm_mod_matmul(xs, norm2_g[i], sc2, sh2, peer_w_query[i].astype(BF16), tm=tm,
                                  tn=peer_w_query.shape[2] // 2, row_of_block=row_tm, n_rows=n_act, with_h=True)
        e2, thr, e1 = peer_scores(qry, peer_sub_keys[i].astype(BF16), tp=tm, n_rows=n_act)
        xs = peer_experts(h2, peer_u[i].astype(BF16), peer_v[i].T.astype(BF16), e2, thr, e1, xs, g2, tp=tm,
                          row_of_block=row_tm, n_rows=n_act)
    return final_norm(xs, final_g, tm=tm, n_rows=n_lat).reshape(nbatch, t_lat, d)
```

```python
import functools
import math

import jax
import jax.numpy as jnp
from jax import lax
from jax.experimental import pallas as pl
from jax.experimental.pallas import tpu as pltpu

F32 = jnp.float32
BF16 = jnp.bfloat16

EPS = 1e-6
N_MOD = 6
MOD_ROWS = 24
LANES = 128
SUBLANES = 8

HEAD_DIM = 128
DN_QK_HEADS = 8
DN_V_HEADS = 16
DN_CONV = 5
DN_CHUNK = 128
DN_HEADS_PER_STEP = 4
ATT_Q_HEADS = 8
ATT_KV_HEADS = 2
ATT_GROUP = ATT_Q_HEADS // ATT_KV_HEADS
GRID_W = 64
ROPE_THETA = 10000.0
PEER_HEADS = 8
PEER_KEYS = 128
PEER_TOPK = 16

VMEM_LIMIT = 56 * 1024 * 1024


def _cparams(sem, flags=None):
    return pltpu.CompilerParams(dimension_semantics=sem, vmem_limit_bytes=VMEM_LIMIT, flags=flags)


def _bdot(a, b):
    return jnp.dot(a.astype(BF16), b.astype(BF16), preferred_element_type=F32)


def _bdot_nt(a, b):
    return lax.dot_general(a.astype(BF16), b.astype(BF16), (((1,), (1,)), ((), ())),
                           preferred_element_type=F32)


def _bdot_tn(a, b):
    return lax.dot_general(a.astype(BF16), b.astype(BF16), (((0,), (0,)), ((), ())),
                           preferred_element_type=F32)


def _split3(x):
    hi = x.astype(BF16)
    r = x - hi.astype(F32)
    mid = r.astype(BF16)
    lo = (r - mid.astype(F32)).astype(BF16)
    return hi, mid, lo


def _exact_dot_01(a01, x):
    a = a01.astype(BF16)
    hi, mid, lo = _split3(x)
    return (jnp.dot(a, hi, preferred_element_type=F32) + jnp.dot(a, mid, preferred_element_type=F32)
            + jnp.dot(a, lo, preferred_element_type=F32))


def _mod_row(nlat_blk, blk_per_batch, nbatch):
    def f(i):
        return jnp.where(i < nlat_blk, i // blk_per_batch, nbatch)
    return f


def _ada_kernel(c_ref, w_ref, b_ref, o_ref):
    c = c_ref[...]
    s = c * jax.nn.sigmoid(c)
    o_ref[0] = _bdot(s, w_ref[0]) + b_ref[0]


def ada_all(cond_rows, ada_w, ada_b):
    depth, d, n = ada_w.shape
    tn = 1536
    return pl.pallas_call(
        _ada_kernel,
        out_shape=jax.ShapeDtypeStruct((depth, MOD_ROWS, n), F32),
        grid=(depth, n // tn),
        in_specs=[pl.BlockSpec((MOD_ROWS, d), lambda l, j: (0, 0)),
                  pl.BlockSpec((1, d, tn), lambda l, j: (l, 0, j)),
                  pl.BlockSpec((1, 1, tn), lambda l, j: (l, 0, j))],
        out_specs=pl.BlockSpec((1, MOD_ROWS, tn), lambda l, j: (l, 0, j)),
        compiler_params=_cparams(("parallel", "parallel")),
        name="ada",
    )(cond_rows, ada_w, ada_b.reshape(depth, 1, n))


def _nmm_kernel(x_ref, g_ref, sc_ref, sh_ref, w_ref, o_ref, *rest, with_h):
    if with_h:
        h_ref, hs_ref = rest
    else:
        (hs_ref,) = rest

    @pl.when(pl.program_id(1) == 0)
    def _():
        x = x_ref[...]
        y = x * lax.rsqrt(jnp.mean(x * x, axis=-1, keepdims=True) + EPS) * g_ref[...]
        h = (y * (1.0 + sc_ref[0]) + sh_ref[0]).astype(BF16)
        hs_ref[...] = h
        if with_h:
            h_ref[...] = h

    o_ref[...] = jnp.dot(hs_ref[...], w_ref[...], preferred_element_type=F32)


def norm_mod_matmul(x, gain, scale, shift, w_bf16, *, tm, tn, row_of_block, n_rows=None, with_h=False):
    n, d = x.shape
    n_rows = n if n_rows is None else n_rows
    nout = w_bf16.shape[1]
    out_shape = [jax.ShapeDtypeStruct((n, nout), F32)]
    out_specs = [pl.BlockSpec((tm, tn), lambda i, j: (i, j))]
    if with_h:
        out_shape.append(jax.ShapeDtypeStruct((n, d), BF16))
        out_specs.append(pl.BlockSpec((tm, d), lambda i, j: (i, 0)))
    res = pl.pallas_call(
        functools.partial(_nmm_kernel, with_h=with_h),
        out_shape=out_shape,
        grid=(n_rows // tm, nout // tn),
        in_specs=[pl.BlockSpec((tm, d), lambda i, j: (i, 0)),
                  pl.BlockSpec((1, d), lambda i, j: (0, 0)),
                  pl.BlockSpec((1, 1, d), lambda i, j: (row_of_block(i), 0, 0)),
                  pl.BlockSpec((1, 1, d), lambda i, j: (row_of_block(i), 0, 0)),
                  pl.BlockSpec((d, tn), lambda i, j: (0, j))],
        out_specs=out_specs,
        scratch_shapes=[pltpu.VMEM((tm, d), BF16)],
        compiler_params=_cparams(("parallel", "arbitrary")),
        name="norm_mod_matmul",
    )(x, gain.reshape(1, d), scale, shift, w_bf16)
    return res if with_h else res[0]


def _mm_res_kernel(a_ref, w_ref, x_ref, gate_ref, o_ref):
    y = jnp.dot(a_ref[...].astype(BF16), w_ref[...], preferred_element_type=F32)
    o_ref[...] = x_ref[...] + gate_ref[0] * y


def matmul_residual(a, w_bf16, x, gate, *, tm, row_of_block, n_rows):
    n, d = x.shape
    k = a.shape[1]
    return pl.pallas_call(
        _mm_res_kernel,
        out_shape=jax.ShapeDtypeStruct((n, d), F32),
        grid=(n_rows // tm,),
        in_specs=[pl.BlockSpec((tm, k), lambda i: (i, 0)),
                  pl.BlockSpec((k, d), lambda i: (0, 0)),
                  pl.BlockSpec((tm, d), lambda i: (i, 0)),
                  pl.BlockSpec((1, 1, d), lambda i: (row_of_block(i), 0, 0))],
        out_specs=pl.BlockSpec((tm, d), lambda i: (i, 0)),
        compiler_params=_cparams(("parallel",)),
        name="matmul_residual",
    )(a, w_bf16, x, gate)


def _dn_prep_kernel(x_ref, xp_ref, xn_ref, w_ref, o_ref, *, tm, nlat_blk, blk_per_lat, blk_per_ctx):
    i = pl.program_id(0)
    j = pl.program_id(1)
    is_lat = i < nlat_blk
    pos = jnp.where(is_lat, i % blk_per_lat, (i - nlat_blk) % blk_per_ctx)
    nblk = jnp.where(is_lat, blk_per_lat, blk_per_ctx)
    x = x_ref[...]
    xp = jnp.where(pos > 0, xp_ref[...], 0.0)
    xn = jnp.where(pos < nblk - 1, xn_ref[...], 0.0)
    xe = jnp.concatenate([xp, x, xn], axis=0)
    half = DN_CONV // 2
    y = None
    for t in range(DN_CONV):
        lo = SUBLANES - half + t
        term = w_ref[t:t + 1, :] * xe[lo:lo + tm, :]
        y = term if y is None else y + term
    y = y * jax.nn.sigmoid(y)

    @pl.when(j < 2)
    def _():
        post = jnp.where(j == 0, HEAD_DIM ** -0.5, 1.0)
        for hh in range(y.shape[1] // HEAD_DIM):
            seg = y[:, hh * HEAD_DIM:(hh + 1) * HEAD_DIM]
            inv = lax.rsqrt(jnp.sum(seg * seg, axis=-1, keepdims=True) + EPS)
            o_ref[:, hh * HEAD_DIM:(hh + 1) * HEAD_DIM] = seg * inv * post

    @pl.when(j >= 2)
    def _():
        o_ref[...] = y


def dn_prep(proj, conv_w, *, tm, nlat_blk, blk_per_lat, blk_per_ctx):
    n = proj.shape[0]
    tc = DN_QK_HEADS * HEAD_DIM
    ncol = conv_w.shape[1] // tc
    sub = tm // SUBLANES
    last = n // SUBLANES - 1
    return pl.pallas_call(
        functools.partial(_dn_prep_kernel, tm=tm, nlat_blk=nlat_blk, blk_per_lat=blk_per_lat,
                          blk_per_ctx=blk_per_ctx),
        out_shape=jax.ShapeDtypeStruct((n, conv_w.shape[1]), F32),
        grid=(n // tm, ncol),
        in_specs=[pl.BlockSpec((tm, tc), lambda i, j: (i, j)),
                  pl.BlockSpec((SUBLANES, tc), lambda i, j: (jnp.maximum(i * sub - 1, 0), j)),
                  pl.BlockSpec((SUBLANES, tc), lambda i, j: (jnp.minimum((i + 1) * sub, last), j)),
                  pl.BlockSpec((DN_CONV, tc), lambda i, j: (0, j))],
        out_specs=pl.BlockSpec((tm, tc), lambda i, j: (i, j)),
        compiler_params=_cparams(("parallel", "parallel")),
        name="dn_prep",
    )(proj, proj, proj, conv_w)


def _dn_gates_kernel(x_ref, alog_ref, dtb_ref, perm_ref, rows_ref, cols_ref, *, tm):
    x = x_ref[...]
    lane = lax.broadcasted_iota(jnp.int32, x.shape, 1)
    is_a = (lane % 32) >= DN_V_HEADS
    is_bwd = lane >= 32
    beta = jax.nn.sigmoid(x)
    g = -jnp.exp(alog_ref[...]) * jax.nn.softplus(x + dtb_ref[...])
    c = DN_CHUNK
    ii = lax.broadcasted_iota(jnp.int32, (c, c), 0)
    jj = lax.broadcasted_iota(jnp.int32, (c, c), 1)
    lower = (ii >= jj).astype(F32)
    upper = (ii <= jj).astype(F32)
    chunk_is_bwd = lax.broadcasted_iota(jnp.int32, (c, x.shape[1]), 1) >= 32
    parts = []
    for ch in range(tm // c):
        gch = g[ch * c:(ch + 1) * c, :]
        parts.append(jnp.where(chunk_is_bwd, _exact_dot_01(upper, gch), _exact_dot_01(lower, gch)))
    gcum = jnp.concatenate(parts, axis=0) if len(parts) > 1 else parts[0]
    vals = jnp.where(is_a, gcum, beta)
    hi, mid, lo = _split3(vals)
    p = perm_ref[...]
    grouped = (jnp.dot(hi, p, preferred_element_type=F32) + jnp.dot(mid, p, preferred_element_type=F32)
               + jnp.dot(lo, p, preferred_element_type=F32))
    gt = grouped.T
    for hq in range(DN_QK_HEADS):
        cols_ref[hq] = grouped[:, hq * SUBLANES:(hq + 1) * SUBLANES]
        rows_ref[hq] = gt[hq * SUBLANES:(hq + 1) * SUBLANES, :]


def dn_gates(proj, a_log, dt_bias, *, tm, gate_col_block):
    n = proj.shape[0]
    zeros16 = jnp.zeros((2, DN_V_HEADS), F32)
    alog = jnp.pad(jnp.stack([zeros16, a_log], axis=1).reshape(1, 64), ((0, 0), (0, 64)))
    dtb = jnp.pad(jnp.stack([zeros16, dt_bias], axis=1).reshape(1, 64), ((0, 0), (0, 64)))
    src = jnp.arange(64)
    d_, kind, hv = src // 32, (src % 32) // 16, src % 16
    dst = (hv // 2) * 8 + d_ * 4 + kind * 2 + (hv % 2)
    perm = jnp.zeros((LANES, LANES), BF16).at[src, dst].set(1.0)
    return pl.pallas_call(
        functools.partial(_dn_gates_kernel, tm=tm),
        out_shape=[jax.ShapeDtypeStruct((DN_QK_HEADS, SUBLANES, n), F32),
                   jax.ShapeDtypeStruct((DN_QK_HEADS, n, SUBLANES), F32)],
        grid=(n // tm,),
        in_specs=[pl.BlockSpec((tm, LANES), lambda i: (i, gate_col_block)),
                  pl.BlockSpec((1, LANES), lambda i: (0, 0)),
                  pl.BlockSpec((1, LANES), lambda i: (0, 0)),
                  pl.BlockSpec((LANES, LANES), lambda i: (0, 0))],
        out_specs=[pl.BlockSpec((DN_QK_HEADS, SUBLANES, tm), lambda i: (0, 0, i)),
                   pl.BlockSpec((DN_QK_HEADS, tm, SUBLANES), lambda i: (0, i, 0))],
        compiler_params=_cparams(("parallel",)),
        name="dn_gates",
    )(proj, alog, dtb, perm)


def _dot_split(a, b):
    a_hi = a.astype(BF16)
    a_lo = (a - a_hi.astype(F32)).astype(BF16)
    b_hi = b.astype(BF16)
    b_lo = (b - b_hi.astype(F32)).astype(BF16)
    return (jnp.dot(a_hi, b_hi, preferred_element_type=F32) + jnp.dot(a_hi, b_lo, preferred_element_type=F32)
            + jnp.dot(a_lo, b_hi, preferred_element_type=F32))


TRI_BASE = 8


def _tri_inverse_minus_eye(nms, xr):
    c = nms[0].shape[0]
    k = int(math.log2(TRI_BASE))
    base = (xr >> k) == 0
    nds = [jnp.where(base, nm, 0.0) for nm in nms]
    xs = [-nd for nd in nds]
    pws = nds
    for _ in range(k - 1):
        pws = [_bdot(pw, pw) for pw in pws]
        xs = [x + pw + _bdot(x, pw) for x, pw in zip(xs, pws)]
    while (1 << k) < c:
        sib = (xr >> k) == 1
        cms = [jnp.where(sib, nm, 0.0) for nm in nms]
        ts = [cm + _bdot(x, cm) for x, cm in zip(xs, cms)]
        xs = [x - (t + _bdot(t, x)) for x, t in zip(xs, ts)]
        k += 1
    return xs


def _delta_chunks(qs, ks, chains, c):
    ii = lax.broadcasted_iota(jnp.int32, (c, c), 0)
    jj = lax.broadcasted_iota(jnp.int32, (c, c), 1)
    xr = ii ^ jj
    incl = (ii >= jj, ii <= jj)
    strict = (ii > jj, ii < jj)
    kks = [_bdot_nt(k, k) for k in ks]
    qks = [_bdot_nt(q, k) for q, k in zip(qs, ks)]
    decays = [jnp.exp(jnp.where(incl[ch["d"]], ch["gcol"] - ch["grow"], -jnp.inf)) for ch in chains]
    nms = [jnp.where(strict[ch["d"]], kks[ch["qk"]] * dec * ch["bcol"], 0.0) for ch, dec in zip(chains, decays)]
    xs = _tri_inverse_minus_eye(nms, xr)
    egcs = [jnp.exp(ch["gcol"]) for ch in chains]
    rhss = [jnp.concatenate([ch["v"] * ch["bcol"], ks[ch["qk"]] * (ch["bcol"] * egc)], axis=1)
            for ch, egc in zip(chains, egcs)]
    sols = [rhs + _bdot(x, rhs) for x, rhs in zip(xs, rhss)]
    dv = chains[0]["v"].shape[1]
    v_news = [sol[:, :dv] - _bdot(sol[:, dv:], ch["s"]) for sol, ch in zip(sols, chains)]
    qkms = [jnp.where(incl[ch["d"]], qks[ch["qk"]] * dec, 0.0) for ch, dec in zip(chains, decays)]
    outs = [_bdot(qs[ch["qk"]] * egc, ch["s"]) + _bdot(qkm, vn)
            for ch, egc, qkm, vn in zip(chains, egcs, qkms, v_news)]
    g_lasts = [ch["gcol"][c - 1:c, :] if ch["d"] == 0 else ch["gcol"][0:1, :] for ch in chains]
    s_news = [ch["s"] * jnp.exp(gl) + _bdot_tn(ks[ch["qk"]] * jnp.exp(gl - ch["gcol"]), vn)
              for ch, gl, vn in zip(chains, g_lasts, v_news)]
    return outs, s_news


def _dn_scan_kernel(qf_ref, kf_ref, vf_ref, rf_ref, cf_ref, qb_ref, kb_ref, vb_ref, rb_ref, cb_ref,
                    of_ref, ob_ref, s_ref):
    @pl.when(pl.program_id(2) == 0)
    def _():
        s_ref[...] = jnp.zeros_like(s_ref)

    rep = DN_V_HEADS // DN_QK_HEADS
    c = qf_ref.shape[0]
    hps = qf_ref.shape[1] // HEAD_DIM
    dirs = ((qf_ref, kf_ref, vf_ref, rf_ref, cf_ref), (qb_ref, kb_ref, vb_ref, rb_ref, cb_ref))
    qs, ks, chains = [], [], []
    for d, (q_ref, k_ref, v_ref, r_ref, c_ref) in enumerate(dirs):
        for hh in range(hps):
            qs.append(q_ref[:, hh * HEAD_DIM:(hh + 1) * HEAD_DIM])
            ks.append(k_ref[:, hh * HEAD_DIM:(hh + 1) * HEAD_DIM])
            rows = r_ref[hh]
            cols = c_ref[hh]
            for r in range(rep):
                ib = d * 4 + r
                ig = d * 4 + 2 + r
                col = (hh * rep + r) * HEAD_DIM
                chains.append(dict(d=d, qk=d * hps + hh, col=col, v=v_ref[:, col:col + HEAD_DIM],
                                   bcol=cols[:, ib:ib + 1], gcol=cols[:, ig:ig + 1], grow=rows[ig:ig + 1, :],
                                   s=s_ref[len(chains)]))
    outs, s_news = _delta_chunks(qs, ks, chains, c)
    for i, (ch, o, s_new) in enumerate(zip(chains, outs, s_news)):
        (of_ref, ob_ref)[ch["d"]][:, ch["col"]:ch["col"] + HEAD_DIM] = o
        s_ref[i] = s_new


def dn_scan(qkv, rows, cols, *, nbatch, t_lat, t_ctx):
    n = qkv.shape[0]
    c = DN_CHUNK
    nc, nl = t_ctx // c, t_lat // c
    lat0 = lambda b: b * nl
    ctx0 = lambda b: (nbatch * t_lat) // c + b * nc

    def fwd(b, s):
        return jnp.where(s < nc, ctx0(b) + s, lat0(b) + (s - nc))

    def bwd(b, s):
        return jnp.where(s < nc, ctx0(b) + (nc - 1 - s), lat0(b) + (nl - 1 - (s - nc)))

    hps = DN_HEADS_PER_STEP
    rep = DN_V_HEADS // DN_QK_HEADS
    qw = HEAD_DIM * hps
    vw = qw * rep
    k_col0 = DN_QK_HEADS // hps
    v_col0 = 2 * DN_QK_HEADS * HEAD_DIM // vw

    def specs(blk):
        return [pl.BlockSpec((c, qw), lambda b, h, s: (blk(b, s), h)),
                pl.BlockSpec((c, qw), lambda b, h, s: (blk(b, s), k_col0 + h)),
                pl.BlockSpec((c, vw), lambda b, h, s: (blk(b, s), v_col0 + h)),
                pl.BlockSpec((hps, SUBLANES, c), lambda b, h, s: (h, 0, blk(b, s))),
                pl.BlockSpec((hps, c, SUBLANES), lambda b, h, s: (h, blk(b, s), 0))]

    ow = DN_V_HEADS * HEAD_DIM
    return pl.pallas_call(
        _dn_scan_kernel,
        out_shape=[jax.ShapeDtypeStruct((n, ow), F32), jax.ShapeDtypeStruct((n, ow), F32)],
        grid=(nbatch, DN_QK_HEADS // hps, nc + nl),
        in_specs=specs(fwd) + specs(bwd),
        out_specs=[pl.BlockSpec((c, vw), lambda b, h, s: (fwd(b, s), h)),
                   pl.BlockSpec((c, vw), lambda b, h, s: (bwd(b, s), h))],
        scratch_shapes=[pltpu.VMEM((2 * hps * rep, HEAD_DIM, HEAD_DIM), F32)],
        compiler_params=_cparams(("parallel", "parallel", "arbitrary")),
        name="dn_scan",
    )(qkv, qkv, qkv, rows, cols, qkv, qkv, qkv, rows, cols)


def _dn_out_kernel(of_ref, ob_ref, z_ref, ng_ref, w_ref, x_ref, gate_ref, o_ref):
    o = of_ref[...] + ob_ref[...]
    z = z_ref[...]
    parts = []
    for hh in range(DN_V_HEADS):
        seg = o[:, hh * HEAD_DIM:(hh + 1) * HEAD_DIM]
        y = seg * lax.rsqrt(jnp.mean(seg * seg, axis=-1, keepdims=True) + EPS) * ng_ref[...]
        zz = z[:, hh * HEAD_DIM:(hh + 1) * HEAD_DIM]
        parts.append((y * (zz * jax.nn.sigmoid(zz))).astype(BF16))
    a = jnp.concatenate(parts, axis=1)
    o_ref[...] = x_ref[...] + gate_ref[0] * jnp.dot(a, w_ref[...], preferred_element_type=F32)


def dn_out(o_f, o_b, proj, norm_g, w_bf16, x, gate, *, tm, row_of_block, z_col_block):
    n, d = x.shape
    k = o_f.shape[1]
    return pl.pallas_call(
        _dn_out_kernel,
        out_shape=jax.ShapeDtypeStruct((n, d), F32),
        grid=(n // tm,),
        in_specs=[pl.BlockSpec((tm, k), lambda i: (i, 0)),
                  pl.BlockSpec((tm, k), lambda i: (i, 0)),
                  pl.BlockSpec((tm, k), lambda i: (i, z_col_block)),
                  pl.BlockSpec((1, HEAD_DIM), lambda i: (0, 0)),
                  pl.BlockSpec((k, d), lambda i: (0, 0)),
                  pl.BlockSpec((tm, d), lambda i: (i, 0)),
                  pl.BlockSpec((1, 1, d), lambda i: (row_of_block(i), 0, 0))],
        out_specs=pl.BlockSpec((tm, d), lambda i: (i, 0)),
        compiler_params=_cparams(("parallel",)),
        name="dn_out",
    )(o_f, o_b, proj, norm_g.reshape(1, HEAD_DIM), w_bf16, x, gate)


def _swap_pairs(x):
    lane = lax.broadcasted_iota(jnp.int32, x.shape, 1)
    nxt = pltpu.roll(x, x.shape[1] - 1, 1)
    prv = pltpu.roll(x, 1, 1)
    return jnp.where(lane % 2 == 0, nxt, prv)


def _att_prep_kernel(p_ref, qg_ref, kg_ref, cos_ref, sin_ref, q_ref, k_ref, v_ref):
    cos = cos_ref[...]
    sin = sin_ref[...]

    def norm_rope(seg, gain):
        y = seg * lax.rsqrt(jnp.mean(seg * seg, axis=-1, keepdims=True) + EPS) * gain
        return (y * cos + _swap_pairs(y) * sin).astype(BF16)

    qw = ATT_Q_HEADS * HEAD_DIM
    for hh in range(ATT_Q_HEADS):
        q_ref[:, hh * HEAD_DIM:(hh + 1) * HEAD_DIM] = norm_rope(
            p_ref[:, hh * HEAD_DIM:(hh + 1) * HEAD_DIM], qg_ref[...])
    for hh in range(ATT_KV_HEADS):
        k_ref[:, hh * HEAD_DIM:(hh + 1) * HEAD_DIM] = norm_rope(
            p_ref[:, qw + hh * HEAD_DIM:qw + (hh + 1) * HEAD_DIM], kg_ref[...])
    kvw = ATT_KV_HEADS * HEAD_DIM
    v_ref[...] = p_ref[:, qw + kvw:qw + 2 * kvw].astype(BF16)


def att_prep(proj, qn_g, kn_g, cos2, sin2, *, tm, nlat_blk, blk_per_lat):
    n, pw = proj.shape
    qw = ATT_Q_HEADS * HEAD_DIM
    kvw = ATT_KV_HEADS * HEAD_DIM
    ident_blk = blk_per_lat
    tab = lambda i: (jnp.where(i < nlat_blk, i % blk_per_lat, ident_blk), 0)
    return pl.pallas_call(
        _att_prep_kernel,
        out_shape=[jax.ShapeDtypeStruct((n, qw), BF16), jax.ShapeDtypeStruct((n, kvw), BF16),
                   jax.ShapeDtypeStruct((n, kvw), BF16)],
        grid=(n // tm,),
        in_specs=[pl.BlockSpec((tm, pw), lambda i: (i, 0)),
                  pl.BlockSpec((1, HEAD_DIM), lambda i: (0, 0)),
                  pl.BlockSpec((1, HEAD_DIM), lambda i: (0, 0)),
                  pl.BlockSpec((tm, HEAD_DIM), tab),
                  pl.BlockSpec((tm, HEAD_DIM), tab)],
        out_specs=[pl.BlockSpec((tm, qw), lambda i: (i, 0)),
                   pl.BlockSpec((tm, kvw), lambda i: (i, 0)),
                   pl.BlockSpec((tm, kvw), lambda i: (i, 0))],
        compiler_params=_cparams(("parallel",)),
        name="att_prep",
    )(proj, qn_g.reshape(1, HEAD_DIM), kn_g.reshape(1, HEAD_DIM), cos2, sin2)


def rope_tables(t_lat, tm):
    rows = t_lat // GRID_W
    row = jnp.broadcast_to(jnp.arange(rows)[:, None], (rows, GRID_W)).reshape(-1).astype(F32)
    col = jnp.broadcast_to(jnp.arange(GRID_W)[None, :], (rows, GRID_W)).reshape(-1).astype(F32)
    axis_dim = HEAD_DIM // 2
    freqs = ROPE_THETA ** (-jnp.arange(0, axis_dim, 2, dtype=F32) / axis_dim)
    ang = jnp.concatenate([row[:, None] * freqs, col[:, None] * freqs], axis=-1)
    cos, sin = jnp.cos(ang), jnp.sin(ang)
    cos2 = jnp.repeat(cos, 2, axis=-1)
    sin2 = jnp.stack([-sin, sin], axis=-1).reshape(t_lat, HEAD_DIM)
    cos2 = jnp.concatenate([cos2, jnp.ones((tm, HEAD_DIM), F32)], axis=0)
    sin2 = jnp.concatenate([sin2, jnp.zeros((tm, HEAD_DIM), F32)], axis=0)
    return cos2, sin2


def _attn_kernel(q_ref, kc_ref, vc_ref, *rest, with_lat, kchunk):
    if with_lat:
        kl_ref, vl_ref, o_ref = rest
    else:
        _, o_ref = rest
    tq = q_ref.shape[0]
    q = jnp.concatenate([q_ref[:, r * HEAD_DIM:(r + 1) * HEAD_DIM] for r in range(ATT_GROUP)], axis=0)
    scale = HEAD_DIM ** -0.5
    pieces = [(kc_ref, vc_ref, 0, kc_ref.shape[0])]
    if with_lat:
        for c0 in range(0, kl_ref.shape[0], kchunk):
            pieces.append((kl_ref, vl_ref, c0, kchunk))
    m = l = acc = None
    for k_ref, v_ref, c0, cn in pieces:
        s = lax.dot_general(q, k_ref[c0:c0 + cn, :], (((1,), (1,)), ((), ())),
                            preferred_element_type=F32) * scale
        mx = jnp.max(s, axis=-1, keepdims=True)
        if m is None:
            m = mx
            p = jnp.exp(s - m)
            l = jnp.sum(p, axis=-1, keepdims=True)
            acc = jnp.dot(p.astype(BF16), v_ref[c0:c0 + cn, :], preferred_element_type=F32)
        else:
            m_new = jnp.maximum(m, mx)
            alpha = jnp.exp(m - m_new)
            p = jnp.exp(s - m_new)
            l = alpha * l + jnp.sum(p, axis=-1, keepdims=True)
            acc = alpha * acc + jnp.dot(p.astype(BF16), v_ref[c0:c0 + cn, :], preferred_element_type=F32)
            m = m_new
    o = acc / l
    for r in range(ATT_GROUP):
        o_ref[:, r * HEAD_DIM:(r + 1) * HEAD_DIM] = o[r * tq:(r + 1) * tq, :].astype(BF16)


def attention(qn, kn, vn, o_lat=None, *, nbatch, t_lat, t_ctx, tq):
    latent_queries = o_lat is None
    n = qn.shape[0]
    gw = ATT_GROUP * HEAD_DIM
    lat_rows = nbatch * t_lat
    ctx_blk = lambda b: (lat_rows // t_ctx) + b
    if latent_queries:
        qblk = lambda b, qi: b * (t_lat // tq) + qi
        nq = t_lat // tq
    else:
        qblk = lambda b, qi: lat_rows // tq + b * (t_ctx // tq) + qi
        nq = t_ctx // tq
    in_specs = [pl.BlockSpec((tq, gw), lambda b, g, qi: (qblk(b, qi), g)),
                pl.BlockSpec((t_ctx, HEAD_DIM), lambda b, g, qi: (ctx_blk(b), g)),
                pl.BlockSpec((t_ctx, HEAD_DIM), lambda b, g, qi: (ctx_blk(b), g))]
    args = [qn, kn, vn]
    if latent_queries:
        in_specs += [pl.BlockSpec((t_lat, HEAD_DIM), lambda b, g, qi: (b, g)),
                     pl.BlockSpec((t_lat, HEAD_DIM), lambda b, g, qi: (b, g))]
        args += [kn, vn]
        aliases = {}
    else:
        in_specs.append(pl.BlockSpec(memory_space=pl.ANY))
        args.append(o_lat)
        aliases = {len(args) - 1: 0}
    kchunk = min(t_lat, 1024)
    return pl.pallas_call(
        functools.partial(_attn_kernel, with_lat=latent_queries, kchunk=kchunk),
        out_shape=jax.ShapeDtypeStruct((n, ATT_Q_HEADS * HEAD_DIM), BF16),
        grid=(nbatch, ATT_KV_HEADS, nq),
        in_specs=in_specs,
        out_specs=pl.BlockSpec((tq, gw), lambda b, g, qi: (qblk(b, qi), g)),
        input_output_aliases=aliases,
        compiler_params=_cparams(("parallel", "parallel", "arbitrary")),
        name="attention_lat" if latent_queries else "attention_ctx",
    )(*args)


N_CAND = PEER_TOPK + 1


def _top_values(ref, s, n):
    cur = s
    for r in range(n):
        m = jnp.max(cur, axis=0, keepdims=True)
        ref[r:r + 1, :] = m
        cur = jnp.where(cur >= m, -jnp.inf, cur)


def _peer_scores_kernel(q_ref, sk_ref, e2_ref, thr_ref, e1_ref, a_ref, b_ref):
    p = q_ref.shape[0]
    neg = -jnp.inf
    a_ref[...] = jnp.full(a_ref.shape, neg, F32)
    b_ref[...] = jnp.full(b_ref.shape, neg, F32)
    for h in range(PEER_HEADS):
        s1 = _bdot_nt(sk_ref[h, 0], q_ref[:, (2 * h) * PEER_KEYS:(2 * h + 1) * PEER_KEYS])
        s2 = _bdot_nt(sk_ref[h, 1], q_ref[:, (2 * h + 1) * PEER_KEYS:(2 * h + 2) * PEER_KEYS])
        _top_values(a_ref, s1, N_CAND)
        _top_values(b_ref, s2, N_CAND)
        a = a_ref[...]
        b = b_ref[...]
        a0, b0 = a[0:1, :], b[0:1, :]
        cands = [a0 + b, a[1:N_CAND, :] + b0]
        ridx = lax.broadcasted_iota(jnp.int32, (SUBLANES, p), 0)
        for i in range(1, SUBLANES):
            nj = N_CAND // (i + 1) - 1
            if nj < 1:
                break
            cands.append(jnp.where(ridx < nj, a[i:i + 1, :] + b[1:1 + SUBLANES, :], neg))
        cand = jnp.concatenate(cands, axis=0)
        cur = cand
        c_prev = None
        for r in range(N_CAND):
            m = jnp.max(cur, axis=0, keepdims=True)
            if r == N_CAND - 2:
                c_prev = m
            cur = jnp.where(cur >= m, neg, cur)
        tau = 0.5 * (c_prev + m)
        top = a0 + b0
        z = jnp.sum(jnp.where(cand > tau, jnp.exp(cand - top), 0.0), axis=0, keepdims=True)
        e2_ref[0, h] = jnp.exp(s2 - b0)
        thr_ref[0, h] = jnp.exp((tau - b0) - s1)
        e1_ref[0, h] = jnp.exp(s1 - a0) * (0.5 / z)


def peer_scores(qry, sub_keys_bf16, *, tp, n_rows):
    nblk = n_rows // tp
    shp = jax.ShapeDtypeStruct((nblk, PEER_HEADS, PEER_KEYS, tp), F32)
    spec = pl.BlockSpec((1, PEER_HEADS, PEER_KEYS, tp), lambda i: (i, 0, 0, 0))
    return pl.pallas_call(
        _peer_scores_kernel,
        out_shape=[shp, shp, shp],
        grid=(nblk,),
        in_specs=[pl.BlockSpec((tp, qry.shape[1]), lambda i: (i, 0)),
                  pl.BlockSpec(sub_keys_bf16.shape, lambda i: (0, 0, 0, 0))],
        out_specs=[spec, spec, spec],
        scratch_shapes=[pltpu.VMEM((24, tp), F32), pltpu.VMEM((24, tp), F32)],
        compiler_params=_cparams(("parallel",)),
        name="peer_scores",
    )(qry, sub_keys_bf16)


PEER_ROWS_PER_STEP = 8
PEER_SUB = 4
PEER_PAIR = 2
PEER_TILE = 64


def _peer_expert_kernel(h_ref, u_ref, vt_ref, e2_ref, thr_ref, e1_ref, x_ref, gate_ref, o_ref, acc_ref, act_ref,
                        coef_ref):
    e = pl.program_id(1)

    @pl.when(e == 0)
    def _():
        acc_ref[...] = jnp.zeros_like(acc_ref)

    tp = h_ref.shape[0]
    h = h_ref[...]
    nsub = PEER_ROWS_PER_STEP // PEER_SUB
    width = PEER_SUB * PEER_KEYS
    inv_sqrt2 = 1.0 / math.sqrt(2.0)
    for sb in range(nsub):
        act_ref[...] = lax.dot_general(u_ref[sb * width:(sb + 1) * width, :], h, (((1,), (1,)), ((), ())),
                                       preferred_element_type=F32)
        ntile = PEER_KEYS // PEER_TILE
        for tl in range(tp // LANES):
            ls = slice(tl * LANES, (tl + 1) * LANES)
            for r0 in range(0, PEER_SUB, PEER_PAIR):
                ms = [[None] * ntile for _ in range(PEER_PAIR)]
                for hd in range(PEER_HEADS):
                    rows = [sb * PEER_SUB + r0 + r for r in range(PEER_PAIR)]
                    thr = [thr_ref[0, hd, row:row + 1, ls] for row in rows]
                    e1 = [e1_ref[0, hd, row:row + 1, ls] for row in rows]
                    for t2 in range(ntile):
                        e2 = e2_ref[0, hd, t2 * PEER_TILE:(t2 + 1) * PEER_TILE, ls]
                        for r in range(PEER_PAIR):
                            term = jnp.where(e2 >= thr[r], e2, 0.0) * e1[r]
                            ms[r][t2] = term if ms[r][t2] is None else ms[r][t2] + term
                for r in range(PEER_PAIR):
                    for t2 in range(ntile):
                        lo = (r0 + r) * PEER_KEYS + t2 * PEER_TILE
                        a = act_ref[lo:lo + PEER_TILE, ls]
                        coef_ref[lo:lo + PEER_TILE, ls] = (
                            ms[r][t2] * (a * (1.0 + lax.erf(a * inv_sqrt2)))).astype(BF16)
        acc_ref[...] += jnp.dot(vt_ref[:, sb * width:(sb + 1) * width], coef_ref[...],
                                preferred_element_type=F32)

    @pl.when(e == pl.num_programs(1) - 1)
    def _():
        o_ref[...] = x_ref[...] + gate_ref[0] * acc_ref[...].T


def peer_experts(h_bf16, u_bf16, vt_bf16, e2, thr, e1, x, gate, *, tp, row_of_block, n_rows):
    n, d = x.shape
    n_exp = u_bf16.shape[0]
    rows = PEER_ROWS_PER_STEP
    ew = rows * PEER_KEYS
    sspec = pl.BlockSpec((1, PEER_HEADS, PEER_KEYS, tp), lambda i, e: (i, 0, 0, 0))
    rspec = pl.BlockSpec((1, PEER_HEADS, rows, tp), lambda i, e: (i, 0, e, 0))
    return pl.pallas_call(
        _peer_expert_kernel,
        out_shape=jax.ShapeDtypeStruct((n, d), F32),
        grid=(n_rows // tp, n_exp // ew),
        in_specs=[pl.BlockSpec((tp, d), lambda i, e: (i, 0)),
                  pl.BlockSpec((ew, d), lambda i, e: (e, 0)),
                  pl.BlockSpec((d, ew), lambda i, e: (0, e)),
                  sspec, rspec, rspec,
                  pl.BlockSpec((tp, d), lambda i, e: (i, 0)),
                  pl.BlockSpec((1, 1, d), lambda i, e: (row_of_block(i), 0, 0))],
        out_specs=pl.BlockSpec((tp, d), lambda i, e: (i, 0)),
        scratch_shapes=[pltpu.VMEM((d, tp), F32), pltpu.VMEM((PEER_SUB * PEER_KEYS, tp), F32),
                        pltpu.VMEM((PEER_SUB * PEER_KEYS, tp), BF16)],
        compiler_params=_cparams(("parallel", "arbitrary")),
        name="peer_experts",
    )(h_bf16, u_bf16, vt_bf16, e2, thr, e1, x, gate)


def _final_norm_kernel(x_ref, g_ref, o_ref):
    x = x_ref[...]
    o_ref[...] = x * lax.rsqrt(jnp.mean(x * x, axis=-1, keepdims=True) + EPS) * g_ref[...]


def final_norm(x, gain, *, tm, n_rows):
    d = x.shape[1]
    return pl.pallas_call(
        _final_norm_kernel,
        out_shape=jax.ShapeDtypeStruct((n_rows, d), F32),
        grid=(n_rows // tm,),
        in_specs=[pl.BlockSpec((tm, d), lambda i: (i, 0)), pl.BlockSpec((1, d), lambda i: (0, 0))],
        out_specs=pl.BlockSpec((tm, d), lambda i: (i, 0)),
        compiler_params=_cparams(("parallel",)),
        name="final_norm",
    )(x, gain.reshape(1, d))


def kernel(x, c, ctx, c_ctx, ada_w, ada_b, norm1_g, norm2_g, final_g, dn_w_in, dn_conv_w, dn_a_log, dn_dt_bias,
           dn_norm_g, dn_w_out, att_w_in, att_qn_g, att_kn_g, att_w_out, peer_w_query, peer_sub_keys, peer_u,
           peer_v):
    nbatch, t_lat, d = x.shape
    t_ctx = ctx.shape[1]
    depth = ada_w.shape[0]
    n_lat = nbatch * t_lat
    n_all = n_lat + nbatch * t_ctx
    assert nbatch + 1 <= MOD_ROWS

    tm = min(512, t_ctx * nbatch, t_lat)
    ts = min(256, t_ctx)
    assert t_lat % tm == 0 and (nbatch * t_ctx) % tm == 0 and t_lat % ts == 0 and t_ctx % ts == 0
    assert t_lat % DN_CHUNK == 0 and t_ctx % DN_CHUNK == 0 and ts % DN_CHUNK == 0
    row_tm = _mod_row(n_lat // tm, t_lat // tm, nbatch)

    xs = jnp.concatenate([x.reshape(n_lat, d), ctx.reshape(nbatch * t_ctx, d)], axis=0)
    cond_rows = jnp.concatenate([c, c_ctx[None, :], jnp.zeros((MOD_ROWS - nbatch - 1, d), F32)], axis=0)
    mod = ada_all(cond_rows, ada_w, ada_b).reshape(depth, MOD_ROWS, N_MOD, 1, d)
    cos2, sin2 = rope_tables(t_lat, ts)

    dn_cols = dn_w_in.shape[2]
    dn_pad = (-dn_cols) % (7 * LANES)
    for i in range(depth):
        last = i == depth - 1
        sh1, sc1, g1, sh2, sc2, g2 = (mod[i, :, m] for m in range(N_MOD))
        j = i // 2
        n_act = n_lat if last else n_all
        if i % 2 == 0:
            w_in = jnp.pad(dn_w_in[j], ((0, 0), (0, dn_pad))).astype(BF16)
            proj = norm_mod_matmul(xs, norm1_g[i], sc1, sh1, w_in, tm=tm, tn=7 * LANES, row_of_block=row_tm)
            conv_w = dn_conv_w[j]
            qkv = dn_prep(proj, conv_w, tm=ts, nlat_blk=n_lat // ts, blk_per_lat=t_lat // ts,
                          blk_per_ctx=t_ctx // ts)
            conv_cols = conv_w.shape[1]
            val_w = DN_V_HEADS * HEAD_DIM
            rows, cols = dn_gates(proj, dn_a_log[j], dn_dt_bias[j], tm=ts,
                                  gate_col_block=(conv_cols + val_w) // LANES)
            o_f, o_b = dn_scan(qkv, rows, cols, nbatch=nbatch, t_lat=t_lat, t_ctx=t_ctx)
            xs = dn_out(o_f, o_b, proj, dn_norm_g[j], dn_w_out[j].astype(BF16), xs, g1, tm=tm,
                        row_of_block=row_tm, z_col_block=conv_cols // val_w)
        else:
            w_in = att_w_in[j].astype(BF16)
            proj = norm_mod_matmul(xs, norm1_g[i], sc1, sh1, w_in, tm=tm, tn=w_in.shape[1], row_of_block=row_tm)
            qn, kn, vn = att_prep(proj, att_qn_g[j], att_kn_g[j], cos2, sin2, tm=ts, nlat_blk=n_lat // ts,
                                  blk_per_lat=t_lat // ts)
            tq = min(128, t_ctx)
            o = attention(qn, kn, vn, nbatch=nbatch, t_lat=t_lat, t_ctx=t_ctx, tq=tq)
            if not last:
                o = attention(qn, kn, vn, o, nbatch=nbatch, t_lat=t_lat, t_ctx=t_ctx, tq=tq)
            xs = matmul_residual(o, att_w_out[j].astype(BF16), xs, g1, tm=tm, row_of_block=row_tm, n_rows=n_act)
        qry, h2 = norm_mod_matmul(xs, norm2_g[i], sc2, sh2, peer_w_query[i].astype(BF16), tm=tm,
                                  tn=peer_w_query.shape[2] // 2, row_of_block=row_tm, n_rows=n_act, with_h=True)
        e2, thr, e1 = peer_scores(qry, peer_sub_keys[i].astype(BF16), tp=tm, n_rows=n_act)
        xs = peer_experts(h2, peer_u[i].astype(BF16), peer_v[i].T.astype(BF16), e2, thr, e1, xs, g2, tp=tm,
                          row_of_block=row_tm, n_rows=n_act)
    return final_norm(xs, final_g, tm=tm, n_rows=n_lat).reshape(nbatch, t_lat, d)
```

```python
import functools
import math

import jax
import jax.numpy as jnp
from jax import lax
from jax.experimental import pallas as pl
from jax.experimental.pallas import tpu as pltpu

F32 = jnp.float32
BF16 = jnp.bfloat16

EPS = 1e-6
N_MOD = 6
MOD_ROWS = 24
LANES = 128
SUBLANES = 8

HEAD_DIM = 128
DN_QK_HEADS = 8
DN_V_HEADS = 16
DN_CONV = 5
DN_CHUNK = 128
DN_HEADS_PER_STEP = 4
ATT_Q_HEADS = 8
ATT_KV_HEADS = 2
ATT_GROUP = ATT_Q_HEADS // ATT_KV_HEADS
GRID_W = 64
ROPE_THETA = 10000.0
PEER_HEADS = 8
PEER_KEYS = 128
PEER_TOPK = 16

VMEM_LIMIT = 56 * 1024 * 1024


def _cparams(sem, flags=None):
    return pltpu.CompilerParams(dimension_semantics=sem, vmem_limit_bytes=VMEM_LIMIT, flags=flags)


def _bdot(a, b):
    return jnp.dot(a.astype(BF16), b.astype(BF16), preferred_element_type=F32)


def _bdot_nt(a, b):
    return lax.dot_general(a.astype(BF16), b.astype(BF16), (((1,), (1,)), ((), ())),
                           preferred_element_type=F32)


def _bdot_tn(a, b):
    return lax.dot_general(a.astype(BF16), b.astype(BF16), (((0,), (0,)), ((), ())),
                           preferred_element_type=F32)


def _split3(x):
    hi = x.astype(BF16)
    r = x - hi.astype(F32)
    mid = r.astype(BF16)
    lo = (r - mid.astype(F32)).astype(BF16)
    return hi, mid, lo


def _exact_dot_01(a01, x):
    a = a01.astype(BF16)
    hi, mid, lo = _split3(x)
    return (jnp.dot(a, hi, preferred_element_type=F32) + jnp.dot(a, mid, preferred_element_type=F32)
            + jnp.dot(a, lo, preferred_element_type=F32))


def _mod_row(nlat_blk, blk_per_batch, nbatch):
    def f(i):
        return jnp.where(i < nlat_blk, i // blk_per_batch, nbatch)
    return f


def _ada_kernel(c_ref, w_ref, b_ref, o_ref):
    c = c_ref[...]
    s = c * jax.nn.sigmoid(c)
    o_ref[0] = _bdot(s, w_ref[0]) + b_ref[0]


def ada_all(cond_rows, ada_w, ada_b):
    depth, d, n = ada_w.shape
    tn = 1536
    return pl.pallas_call(
        _ada_kernel,
        out_shape=jax.ShapeDtypeStruct((depth, MOD_ROWS, n), F32),
        grid=(depth, n // tn),
        in_specs=[pl.BlockSpec((MOD_ROWS, d), lambda l, j: (0, 0)),
                  pl.BlockSpec((1, d, tn), lambda l, j: (l, 0, j)),
                  pl.BlockSpec((1, 1, tn), lambda l, j: (l, 0, j))],
        out_specs=pl.BlockSpec((1, MOD_ROWS, tn), lambda l, j: (l, 0, j)),
        compiler_params=_cparams(("parallel", "parallel")),
        name="ada",
    )(cond_rows, ada_w, ada_b.reshape(depth, 1, n))


def _nmm_kernel(x_ref, g_ref, sc_ref, sh_ref, w_ref, o_ref, *rest, with_h):
    if with_h:
        h_ref, hs_ref = rest
    else:
        (hs_ref,) = rest

    @pl.when(pl.program_id(1) == 0)
    def _():
        x = x_ref[...]
        y = x * lax.rsqrt(jnp.mean(x * x, axis=-1, keepdims=True) + EPS) * g_ref[...]
        h = (y * (1.0 + sc_ref[0]) + sh_ref[0]).astype(BF16)
        hs_ref[...] = h
        if with_h:
            h_ref[...] = h

    o_ref[...] = jnp.dot(hs_ref[...], w_ref[...], preferred_element_type=F32)


def norm_mod_matmul(x, gain, scale, shift, w_bf16, *, tm, tn, row_of_block, n_rows=None, with_h=False):
    n, d = x.shape
    n_rows = n if n_rows is None else n_rows
    nout = w_bf16.shape[1]
    out_shape = [jax.ShapeDtypeStruct((n, nout), F32)]
    out_specs = [pl.BlockSpec((tm, tn), lambda i, j: (i, j))]
    if with_h:
        out_shape.append(jax.ShapeDtypeStruct((n, d), BF16))
        out_specs.append(pl.BlockSpec((tm, d), lambda i, j: (i, 0)))
    res = pl.pallas_call(
        functools.partial(_nmm_kernel, with_h=with_h),
        out_shape=out_shape,
        grid=(n_rows // tm, nout // tn),
        in_specs=[pl.BlockSpec((tm, d), lambda i, j: (i, 0)),
                  pl.BlockSpec((1, d), lambda i, j: (0, 0)),
                  pl.BlockSpec((1, 1, d), lambda i, j: (row_of_block(i), 0, 0)),
                  pl.BlockSpec((1, 1, d), lambda i, j: (row_of_block(i), 0, 0)),
                  pl.BlockSpec((d, tn), lambda i, j: (0, j))],
        out_specs=out_specs,
        scratch_shapes=[pltpu.VMEM((tm, d), BF16)],
        compiler_params=_cparams(("parallel", "arbitrary")),
        name="norm_mod_matmul",
    )(x, gain.reshape(1, d), scale, shift, w_bf16)
    return res if with_h else res[0]


def _mm_res_kernel(a_ref, w_ref, x_ref, gate_ref, o_ref):
    y = jnp.dot(a_ref[...].astype(BF16), w_ref[...], preferred_element_type=F32)
    o_ref[...] = x_ref[...] + gate_ref[0] * y


def matmul_residual(a, w_bf16, x, gate, *, tm, row_of_block, n_rows):
    n, d = x.shape
    k = a.shape[1]
    return pl.pallas_call(
        _mm_res_kernel,
        out_shape=jax.ShapeDtypeStruct((n, d), F32),
        grid=(n_rows // tm,),
        in_specs=[pl.BlockSpec((tm, k), lambda i: (i, 0)),
                  pl.BlockSpec((k, d), lambda i: (0, 0)),
                  pl.BlockSpec((tm, d), lambda i: (i, 0)),
                  pl.BlockSpec((1, 1, d), lambda i: (row_of_block(i), 0, 0))],
        out_specs=pl.BlockSpec((tm, d), lambda i: (i, 0)),
        compiler_params=_cparams(("parallel",)),
        name="matmul_residual",
    )(a, w_bf16, x, gate)


def _dn_prep_kernel(x_ref, xp_ref, xn_ref, w_ref, o_ref, *, tm, nlat_blk, blk_per_lat, blk_per_ctx):
    i = pl.program_id(0)
    j = pl.program_id(1)
    is_lat = i < nlat_blk
    pos = jnp.where(is_lat, i % blk_per_lat, (i - nlat_blk) % blk_per_ctx)
    nblk = jnp.where(is_lat, blk_per_lat, blk_per_ctx)
    x = x_ref[...]
    xp = jnp.where(pos > 0, xp_ref[...], 0.0)
    xn = jnp.where(pos < nblk - 1, xn_ref[...], 0.0)
    xe = jnp.concatenate([xp, x, xn], axis=0)
    half = DN_CONV // 2
    y = None
    for t in range(DN_CONV):
        lo = SUBLANES - half + t
        term = w_ref[t:t + 1, :] * xe[lo:lo + tm, :]
        y = term if y is None else y + term
    y = y * jax.nn.sigmoid(y)

    @pl.when(j < 2)
    def _():
        post = jnp.where(j == 0, HEAD_DIM ** -0.5, 1.0)
        for hh in range(y.shape[1] // HEAD_DIM):
            seg = y[:, hh * HEAD_DIM:(hh + 1) * HEAD_DIM]
            inv = lax.rsqrt(jnp.sum(seg * seg, axis=-1, keepdims=True) + EPS)
            o_ref[:, hh * HEAD_DIM:(hh + 1) * HEAD_DIM] = seg * inv * post

    @pl.when(j >= 2)
    def _():
        o_ref[...] = y


def dn_prep(proj, conv_w, *, tm, nlat_blk, blk_per_lat, blk_per_ctx):
    n = proj.shape[0]
    tc = DN_QK_HEADS * HEAD_DIM
    ncol = conv_w.shape[1] // tc
    sub = tm // SUBLANES
    last = n // SUBLANES - 1
    return pl.pallas_call(
        functools.partial(_dn_prep_kernel, tm=tm, nlat_blk=nlat_blk, blk_per_lat=blk_per_lat,
                          blk_per_ctx=blk_per_ctx),
        out_shape=jax.ShapeDtypeStruct((n, conv_w.shape[1]), F32),
        grid=(n // tm, ncol),
        in_specs=[pl.BlockSpec((tm, tc), lambda i, j: (i, j)),
                  pl.BlockSpec((SUBLANES, tc), lambda i, j: (jnp.maximum(i * sub - 1, 0), j)),
                  pl.BlockSpec((SUBLANES, tc), lambda i, j: (jnp.minimum((i + 1) * sub, last), j)),
                  pl.BlockSpec((DN_CONV, tc), lambda i, j: (0, j))],
        out_specs=pl.BlockSpec((tm, tc), lambda i, j: (i, j)),
        compiler_params=_cparams(("parallel", "parallel")),
        name="dn_prep",
    )(proj, proj, proj, conv_w)


def _dn_gates_kernel(x_ref, alog_ref, dtb_ref, perm_ref, rows_ref, cols_ref, *, tm):
    x = x_ref[...]
    lane = lax.broadcasted_iota(jnp.int32, x.shape, 1)
    is_a = (lane % 32) >= DN_V_HEADS
    is_bwd = lane >= 32
    beta = jax.nn.sigmoid(x)
    g = -jnp.exp(alog_ref[...]) * jax.nn.softplus(x + dtb_ref[...])
    c = DN_CHUNK
    ii = lax.broadcasted_iota(jnp.int32, (c, c), 0)
    jj = lax.broadcasted_iota(jnp.int32, (c, c), 1)
    lower = (ii >= jj).astype(F32)
    upper = (ii <= jj).astype(F32)
    chunk_is_bwd = lax.broadcasted_iota(jnp.int32, (c, x.shape[1]), 1) >= 32
    parts = []
    for ch in range(tm // c):
        gch = g[ch * c:(ch + 1) * c, :]
        parts.append(jnp.where(chunk_is_bwd, _exact_dot_01(upper, gch), _exact_dot_01(lower, gch)))
    gcum = jnp.concatenate(parts, axis=0) if len(parts) > 1 else parts[0]
    vals = jnp.where(is_a, gcum, beta)
    hi, mid, lo = _split3(vals)
    p = perm_ref[...]
    grouped = (jnp.dot(hi, p, preferred_element_type=F32) + jnp.dot(mid, p, preferred_element_type=F32)
               + jnp.dot(lo, p, preferred_element_type=F32))
    gt = grouped.T
    for hq in range(DN_QK_HEADS):
        cols_ref[hq] = grouped[:, hq * SUBLANES:(hq + 1) * SUBLANES]
        rows_ref[hq] = gt[hq * SUBLANES:(hq + 1) * SUBLANES, :]


def dn_gates(proj, a_log, dt_bias, *, tm, gate_col_block):
    n = proj.shape[0]
    zeros16 = jnp.zeros((2, DN_V_HEADS), F32)
    alog = jnp.pad(jnp.stack([zeros16, a_log], axis=1).reshape(1, 64), ((0, 0), (0, 64)))
    dtb = jnp.pad(jnp.stack([zeros16, dt_bias], axis=1).reshape(1, 64), ((0, 0), (0, 64)))
    src = jnp.arange(64)
    d_, kind, hv = src // 32, (src % 32) // 16, src % 16
    dst = (hv // 2) * 8 + d_ * 4 + kind * 2 + (hv % 2)
    perm = jnp.zeros((LANES, LANES), BF16).at[src, dst].set(1.0)
    return pl.pallas_call(
        functools.partial(_dn_gates_kernel, tm=tm),
        out_shape=[jax.ShapeDtypeStruct((DN_QK_HEADS, SUBLANES, n), F32),
                   jax.ShapeDtypeStruct((DN_QK_HEADS, n, SUBLANES), F32)],
        grid=(n // tm,),
        in_specs=[pl.BlockSpec((tm, LANES), lambda i: (i, gate_col_block)),
                  pl.BlockSpec((1, LANES), lambda i: (0, 0)),
                  pl.BlockSpec((1, LANES), lambda i: (0, 0)),
                  pl.BlockSpec((LANES, LANES), lambda i: (0, 0))],
        out_specs=[pl.BlockSpec((DN_QK_HEADS, SUBLANES, tm), lambda i: (0, 0, i)),
                   pl.BlockSpec((DN_QK_HEADS, tm, SUBLANES), lambda i: (0, i, 0))],
        compiler_params=_cparams(("parallel",)),
        name="dn_gates",
    )(proj, alog, dtb, perm)


def _dot_split(a, b):
    a_hi = a.astype(BF16)
    a_lo = (a - a_hi.astype(F32)).astype(BF16)
    b_hi = b.astype(BF16)
    b_lo = (b - b_hi.astype(F32)).astype(BF16)
    return (jnp.dot(a_hi, b_hi, preferred_element_type=F32) + jnp.dot(a_hi, b_lo, preferred_element_type=F32)
            + jnp.dot(a_lo, b_hi, preferred_element_type=F32))


TRI_BASE = 8


def _tri_inverse_minus_eye(nms, xr):
    c = nms[0].shape[0]
    k = int(math.log2(TRI_BASE))
    base = (xr >> k) == 0
    nds = [jnp.where(base, nm, 0.0) for nm in nms]
    xs = [-nd for nd in nds]
    pws = nds
    for _ in range(k - 1):
        pws = [_bdot(pw, pw) for pw in pws]
        xs = [x + pw + _bdot(x, pw) for x, pw in zip(xs, pws)]
    while (1 << k) < c:
        sib = (xr >> k) == 1
        cms = [jnp.where(sib, nm, 0.0) for nm in nms]
        ts = [cm + _bdot(x, cm) for x, cm in zip(xs, cms)]
        xs = [x - (t + _bdot(t, x)) for x, t in zip(xs, ts)]
        k += 1
    return xs


def _delta_chunks(qs, ks, chains, c):
    ii = lax.broadcasted_iota(jnp.int32, (c, c), 0)
    jj = lax.broadcasted_iota(jnp.int32, (c, c), 1)
    xr = ii ^ jj
    incl = (ii >= jj, ii <= jj)
    strict = (ii > jj, ii < jj)
    kks = [_bdot_nt(k, k) for k in ks]
    qks = [_bdot_nt(q, k) for q, k in zip(qs, ks)]
    decays = [jnp.exp(jnp.where(incl[ch["d"]], ch["gcol"] - ch["grow"], -jnp.inf)) for ch in chains]
    nms = [jnp.where(strict[ch["d"]], kks[ch["qk"]] * dec * ch["bcol"], 0.0) for ch, dec in zip(chains, decays)]
    xs = _tri_inverse_minus_eye(nms, xr)
    egcs = [jnp.exp(ch["gcol"]) for ch in chains]
    rhss = [jnp.concatenate([ch["v"] * ch["bcol"], ks[ch["qk"]] * (ch["bcol"] * egc)], axis=1)
            for ch, egc in zip(chains, egcs)]
    sols = [rhs + _bdot(x, rhs) for x, rhs in zip(xs, rhss)]
    dv = chains[0]["v"].shape[1]
    v_news = [sol[:, :dv] - _bdot(sol[:, dv:], ch["s"]) for sol, ch in zip(sols, chains)]
    qkms = [jnp.where(incl[ch["d"]], qks[ch["qk"]] * dec, 0.0) for ch, dec in zip(chains, decays)]
    outs = [_bdot(qs[ch["qk"]] * egc, ch["s"]) + _bdot(qkm, vn)
            for ch, egc, qkm, vn in zip(chains, egcs, qkms, v_news)]
    g_lasts = [ch["gcol"][c - 1:c, :] if ch["d"] == 0 else ch["gcol"][0:1, :] for ch in chains]
    s_news = [ch["s"] * jnp.exp(gl) + _bdot_tn(ks[ch["qk"]] * jnp.exp(gl - ch["gcol"]), vn)
              for ch, gl, vn in zip(chains, g_lasts, v_news)]
    return outs, s_news


def _dn_scan_kernel(qf_ref, kf_ref, vf_ref, rf_ref, cf_ref, qb_ref, kb_ref, vb_ref, rb_ref, cb_ref,
                    of_ref, ob_ref, s_ref):
    @pl.when(pl.program_id(2) == 0)
    def _():
        s_ref[...] = jnp.zeros_like(s_ref)

    rep = DN_V_HEADS // DN_QK_HEADS
    c = qf_ref.shape[0]
    hps = qf_ref.shape[1] // HEAD_DIM
    dirs = ((qf_ref, kf_ref, vf_ref, rf_ref, cf_ref), (qb_ref, kb_ref, vb_ref, rb_ref, cb_ref))
    qs, ks, chains = [], [], []
    for d, (q_ref, k_ref, v_ref, r_ref, c_ref) in enumerate(dirs):
        for hh in range(hps):
            qs.append(q_ref[:, hh * HEAD_DIM:(hh + 1) * HEAD_DIM])
            ks.append(k_ref[:, hh * HEAD_DIM:(hh + 1) * HEAD_DIM])
            rows = r_ref[hh]
            cols = c_ref[hh]
            for r in range(rep):
                ib = d * 4 + r
                ig = d * 4 + 2 + r
                col = (hh * rep + r) * HEAD_DIM
                chains.append(dict(d=d, qk=d * hps + hh, col=col, v=v_ref[:, col:col + HEAD_DIM],
                                   bcol=cols[:, ib:ib + 1], gcol=cols[:, ig:ig + 1], grow=rows[ig:ig + 1, :],
                                   s=s_ref[len(chains)]))
    outs, s_news = _delta_chunks(qs, ks, chains, c)
    for i, (ch, o, s_new) in enumerate(zip(chains, outs, s_news)):
        (of_ref, ob_ref)[ch["d"]][:, ch["col"]:ch["col"] + HEAD_DIM] = o
        s_ref[i] = s_new


def dn_scan(qkv, rows, cols, *, nbatch, t_lat, t_ctx):
    n = qkv.shape[0]
    c = DN_CHUNK
    nc, nl = t_ctx // c, t_lat // c
    lat0 = lambda b: b * nl
    ctx0 = lambda b: (nbatch * t_lat) // c + b * nc

    def fwd(b, s):
        return jnp.where(s < nc, ctx0(b) + s, lat0(b) + (s - nc))

    def bwd(b, s):
        return jnp.where(s < nc, ctx0(b) + (nc - 1 - s), lat0(b) + (nl - 1 - (s - nc)))

    hps = DN_HEADS_PER_STEP
    rep = DN_V_HEADS // DN_QK_HEADS
    qw = HEAD_DIM * hps
    vw = qw * rep
    k_col0 = DN_QK_HEADS // hps
    v_col0 = 2 * DN_QK_HEADS * HEAD_DIM // vw

    def specs(blk):
        return [pl.BlockSpec((c, qw), lambda b, h, s: (blk(b, s), h)),
                pl.BlockSpec((c, qw), lambda b, h, s: (blk(b, s), k_col0 + h)),
                pl.BlockSpec((c, vw), lambda b, h, s: (blk(b, s), v_col0 + h)),
                pl.BlockSpec((hps, SUBLANES, c), lambda b, h, s: (h, 0, blk(b, s))),
                pl.BlockSpec((hps, c, SUBLANES), lambda b, h, s: (h, blk(b, s), 0))]

    ow = DN_V_HEADS * HEAD_DIM
    return pl.pallas_call(
        _dn_scan_kernel,
        out_shape=[jax.ShapeDtypeStruct((n, ow), F32), jax.ShapeDtypeStruct((n, ow), F32)],
        grid=(nbatch, DN_QK_HEADS // hps, nc + nl),
        in_specs=specs(fwd) + specs(bwd),
        out_specs=[pl.BlockSpec((c, vw), lambda b, h, s: (fwd(b, s), h)),
                   pl.BlockSpec((c, vw), lambda b, h, s: (bwd(b, s), h))],
        scratch_shapes=[pltpu.VMEM((2 * hps * rep, HEAD_DIM, HEAD_DIM), F32)],
        compiler_params=_cparams(("parallel", "parallel", "arbitrary")),
        name="dn_scan",
    )(qkv, qkv, qkv, rows, cols, qkv, qkv, qkv, rows, cols)


def _dn_out_kernel(of_ref, ob_ref, z_ref, ng_ref, w_ref, x_ref, gate_ref, o_ref):
    o = of_ref[...] + ob_ref[...]
    z = z_ref[...]
    parts = []
    for hh in range(DN_V_HEADS):
        seg = o[:, hh * HEAD_DIM:(hh + 1) * HEAD_DIM]
        y = seg * lax.rsqrt(jnp.mean(seg * seg, axis=-1, keepdims=True) + EPS) * ng_ref[...]
        zz = z[:, hh * HEAD_DIM:(hh + 1) * HEAD_DIM]
        parts.append((y * (zz * jax.nn.sigmoid(zz))).astype(BF16))
    a = jnp.concatenate(parts, axis=1)
    o_ref[...] = x_ref[...] + gate_ref[0] * jnp.dot(a, w_ref[...], preferred_element_type=F32)


def dn_out(o_f, o_b, proj, norm_g, w_bf16, x, gate, *, tm, row_of_block, z_col_block):
    n, d = x.shape
    k = o_f.shape[1]
    return pl.pallas_call(
        _dn_out_kernel,
        out_shape=jax.ShapeDtypeStruct((n, d), F32),
        grid=(n // tm,),
        in_specs=[pl.BlockSpec((tm, k), lambda i: (i, 0)),
                  pl.BlockSpec((tm, k), lambda i: (i, 0)),
                  pl.BlockSpec((tm, k), lambda i: (i, z_col_block)),
                  pl.BlockSpec((1, HEAD_DIM), lambda i: (0, 0)),
                  pl.BlockSpec((k, d), lambda i: (0, 0)),
                  pl.BlockSpec((tm, d), lambda i: (i, 0)),
                  pl.BlockSpec((1, 1, d), lambda i: (row_of_block(i), 0, 0))],
        out_specs=pl.BlockSpec((tm, d), lambda i: (i, 0)),
        compiler_params=_cparams(("parallel",)),
        name="dn_out",
    )(o_f, o_b, proj, norm_g.reshape(1, HEAD_DIM), w_bf16, x, gate)


def _swap_pairs(x):
    lane = lax.broadcasted_iota(jnp.int32, x.shape, 1)
    nxt = pltpu.roll(x, x.shape[1] - 1, 1)
    prv = pltpu.roll(x, 1, 1)
    return jnp.where(lane % 2 == 0, nxt, prv)


def _att_prep_kernel(p_ref, qg_ref, kg_ref, cos_ref, sin_ref, q_ref, k_ref, v_ref):
    cos = cos_ref[...]
    sin = sin_ref[...]

    def norm_rope(seg, gain):
        y = seg * lax.rsqrt(jnp.mean(seg * seg, axis=-1, keepdims=True) + EPS) * gain
        return (y * cos + _swap_pairs(y) * sin).astype(BF16)

    qw = ATT_Q_HEADS * HEAD_DIM
    for hh in range(ATT_Q_HEADS):
        q_ref[:, hh * HEAD_DIM:(hh + 1) * HEAD_DIM] = norm_rope(
            p_ref[:, hh * HEAD_DIM:(hh + 1) * HEAD_DIM], qg_ref[...])
    for hh in range(ATT_KV_HEADS):
        k_ref[:, hh * HEAD_DIM:(hh + 1) * HEAD_DIM] = norm_rope(
            p_ref[:, qw + hh * HEAD_DIM:qw + (hh + 1) * HEAD_DIM], kg_ref[...])
    kvw = ATT_KV_HEADS * HEAD_DIM
    v_ref[...] = p_ref[:, qw + kvw:qw + 2 * kvw].astype(BF16)


def att_prep(proj, qn_g, kn_g, cos2, sin2, *, tm, nlat_blk, blk_per_lat):
    n, pw = proj.shape
    qw = ATT_Q_HEADS * HEAD_DIM
    kvw = ATT_KV_HEADS * HEAD_DIM
    ident_blk = blk_per_lat
    tab = lambda i: (jnp.where(i < nlat_blk, i % blk_per_lat, ident_blk), 0)
    return pl.pallas_call(
        _att_prep_kernel,
        out_shape=[jax.ShapeDtypeStruct((n, qw), BF16), jax.ShapeDtypeStruct((n, kvw), BF16),
                   jax.ShapeDtypeStruct((n, kvw), BF16)],
        grid=(n // tm,),
        in_specs=[pl.BlockSpec((tm, pw), lambda i: (i, 0)),
                  pl.BlockSpec((1, HEAD_DIM), lambda i: (0, 0)),
                  pl.BlockSpec((1, HEAD_DIM), lambda i: (0, 0)),
                  pl.BlockSpec((tm, HEAD_DIM), tab),
                  pl.BlockSpec((tm, HEAD_DIM), tab)],
        out_specs=[pl.BlockSpec((tm, qw), lambda i: (i, 0)),
                   pl.BlockSpec((tm, kvw), lambda i: (i, 0)),
                   pl.BlockSpec((tm, kvw), lambda i: (i, 0))],
        compiler_params=_cparams(("parallel",)),
        name="att_prep",
    )(proj, qn_g.reshape(1, HEAD_DIM), kn_g.reshape(1, HEAD_DIM), cos2, sin2)


def rope_tables(t_lat, tm):
    rows = t_lat // GRID_W
    row = jnp.broadcast_to(jnp.arange(rows)[:, None], (rows, GRID_W)).reshape(-1).astype(F32)
    col = jnp.broadcast_to(jnp.arange(GRID_W)[None, :], (rows, GRID_W)).reshape(-1).astype(F32)
    axis_dim = HEAD_DIM // 2
    freqs = ROPE_THETA ** (-jnp.arange(0, axis_dim, 2, dtype=F32) / axis_dim)
    ang = jnp.concatenate([row[:, None] * freqs, col[:, None] * freqs], axis=-1)
    cos, sin = jnp.cos(ang), jnp.sin(ang)
    cos2 = jnp.repeat(cos, 2, axis=-1)
    sin2 = jnp.stack([-sin, sin], axis=-1).reshape(t_lat, HEAD_DIM)
    cos2 = jnp.concatenate([cos2, jnp.ones((tm, HEAD_DIM), F32)], axis=0)
    sin2 = jnp.concatenate([sin2, jnp.zeros((tm, HEAD_DIM), F32)], axis=0)
    return cos2, sin2


def _attn_kernel(q_ref, kc_ref, vc_ref, *rest, with_lat, kchunk):
    if with_lat:
        kl_ref, vl_ref, o_ref = rest
    else:
        _, o_ref = rest
    tq = q_ref.shape[0]
    q = jnp.concatenate([q_ref[:, r * HEAD_DIM:(r + 1) * HEAD_DIM] for r in range(ATT_GROUP)], axis=0)
    scale = HEAD_DIM ** -0.5 * math.log2(math.e)
    pieces = [(kc_ref, vc_ref, 0, kc_ref.shape[0])]
    if with_lat:
        for c0 in range(0, kl_ref.shape[0], kchunk):
            pieces.append((kl_ref, vl_ref, c0, kchunk))
    m = l = acc = None
    for k_ref, v_ref, c0, cn in pieces:
        s = lax.dot_general(q, k_ref[c0:c0 + cn, :], (((1,), (1,)), ((), ())),
                            preferred_element_type=F32) * scale
        mx = jnp.max(s, axis=-1, keepdims=True)
        if m is None:
            m = mx
            p = jnp.exp2(s - m)
            l = jnp.sum(p, axis=-1, keepdims=True)
            acc = jnp.dot(p.astype(BF16), v_ref[c0:c0 + cn, :], preferred_element_type=F32)
        else:
            m_new = jnp.maximum(m, mx)
            alpha = jnp.exp2(m - m_new)
            p = jnp.exp2(s - m_new)
            l = alpha * l + jnp.sum(p, axis=-1, keepdims=True)
            acc = alpha * acc + jnp.dot(p.astype(BF16), v_ref[c0:c0 + cn, :], preferred_element_type=F32)
            m = m_new
    o = acc / l
    for r in range(ATT_GROUP):
        o_ref[:, r * HEAD_DIM:(r + 1) * HEAD_DIM] = o[r * tq:(r + 1) * tq, :].astype(BF16)


def attention(qn, kn, vn, o_lat=None, *, nbatch, t_lat, t_ctx, tq):
    latent_queries = o_lat is None
    n = qn.shape[0]
    gw = ATT_GROUP * HEAD_DIM
    lat_rows = nbatch * t_lat
    ctx_blk = lambda b: (lat_rows // t_ctx) + b
    if latent_queries:
        qblk = lambda b, qi: b * (t_lat // tq) + qi
        nq = t_lat // tq
    else:
        qblk = lambda b, qi: lat_rows // tq + b * (t_ctx // tq) + qi
        nq = t_ctx // tq
    in_specs = [pl.BlockSpec((tq, gw), lambda b, g, qi: (qblk(b, qi), g)),
                pl.BlockSpec((t_ctx, HEAD_DIM), lambda b, g, qi: (ctx_blk(b), g)),
                pl.BlockSpec((t_ctx, HEAD_DIM), lambda b, g, qi: (ctx_blk(b), g))]
    args = [qn, kn, vn]
    if latent_queries:
        in_specs += [pl.BlockSpec((t_lat, HEAD_DIM), lambda b, g, qi: (b, g)),
                     pl.BlockSpec((t_lat, HEAD_DIM), lambda b, g, qi: (b, g))]
        args += [kn, vn]
        aliases = {}
    else:
        in_specs.append(pl.BlockSpec(memory_space=pl.ANY))
        args.append(o_lat)
        aliases = {len(args) - 1: 0}
    kchunk = min(t_lat, 1024)
    return pl.pallas_call(
        functools.partial(_attn_kernel, with_lat=latent_queries, kchunk=kchunk),
        out_shape=jax.ShapeDtypeStruct((n, ATT_Q_HEADS * HEAD_DIM), BF16),
        grid=(nbatch, ATT_KV_HEADS, nq),
        in_specs=in_specs,
        out_specs=pl.BlockSpec((tq, gw), lambda b, g, qi: (qblk(b, qi), g)),
        input_output_aliases=aliases,
        compiler_params=_cparams(("parallel", "parallel", "arbitrary")),
        name="attention_lat" if latent_queries else "attention_ctx",
    )(*args)


N_CAND = PEER_TOPK + 1


def _sorting_network(n):
    pairs = []
    p = 1
    while p < n:
        k = p
        while k >= 1:
            for j in range(k % p, n - k, 2 * k):
                for i in range(min(k, n - j - k)):
                    if (i + j) // (2 * p) == (i + j + k) // (2 * p):
                        pairs.append((i + j, i + j + k))
            k //= 2
        p *= 2
    return pairs


def _top_values(ref, s, n):
    groups = s.shape[0] // SUBLANES
    v = [s[g * SUBLANES:(g + 1) * SUBLANES, :] for g in range(groups)]
    for a, b in _sorting_network(groups):
        hi = jnp.maximum(v[a], v[b])
        lo = jnp.minimum(v[a], v[b])
        v[a], v[b] = hi, lo
    neg = jnp.full_like(v[0], -jnp.inf)
    for r in range(n):
        m = jnp.max(v[0], axis=0, keepdims=True)
        ref[r:r + 1, :] = m
        pop = v[0] >= m
        keep = min(groups, n - 1 - r)
        for j in range(keep):
            nxt = v[j + 1] if j + 1 < groups else neg
            v[j] = jnp.where(pop, nxt, v[j])


def _peer_scores_kernel(q_ref, sk_ref, e2_ref, thr_ref, e1_ref, a_ref, b_ref):
    p = q_ref.shape[0]
    neg = -jnp.inf
    a_ref[...] = jnp.full(a_ref.shape, neg, F32)
    b_ref[...] = jnp.full(b_ref.shape, neg, F32)
    for h in range(PEER_HEADS):
        s1 = _bdot_nt(sk_ref[h, 0], q_ref[:, (2 * h) * PEER_KEYS:(2 * h + 1) * PEER_KEYS])
        s2 = _bdot_nt(sk_ref[h, 1], q_ref[:, (2 * h + 1) * PEER_KEYS:(2 * h + 2) * PEER_KEYS])
        _top_values(a_ref, s1, N_CAND)
        _top_values(b_ref, s2, N_CAND)
        a = a_ref[...]
        b = b_ref[...]
        a0, b0 = a[0:1, :], b[0:1, :]
        cands = [a0 + b, a[1:N_CAND, :] + b0]
        ridx = lax.broadcasted_iota(jnp.int32, (SUBLANES, p), 0)
        for i in range(1, SUBLANES):
            nj = N_CAND // (i + 1) - 1
            if nj < 1:
                break
            cands.append(jnp.where(ridx < nj, a[i:i + 1, :] + b[1:1 + SUBLANES, :], neg))
        cand = jnp.concatenate(cands, axis=0)
        cur = cand
        c_prev = None
        for r in range(N_CAND):
            m = jnp.max(cur, axis=0, keepdims=True)
            if r == N_CAND - 2:
                c_prev = m
            cur = jnp.where(cur >= m, neg, cur)
        tau = 0.5 * (c_prev + m)
        top = a0 + b0
        z = jnp.sum(jnp.where(cand > tau, jnp.exp(cand - top), 0.0), axis=0, keepdims=True)
        e2_ref[0, h] = jnp.exp(s2 - b0)
        thr_ref[0, h] = jnp.exp((tau - b0) - s1)
        e1_ref[0, h] = jnp.exp(s1 - a0) * (0.5 / z)


def peer_scores(qry, sub_keys_bf16, *, tp, n_rows):
    nblk = n_rows // tp
    shp = jax.ShapeDtypeStruct((nblk, PEER_HEADS, PEER_KEYS, tp), F32)
    spec = pl.BlockSpec((1, PEER_HEADS, PEER_KEYS, tp), lambda i: (i, 0, 0, 0))
    return pl.pallas_call(
        _peer_scores_kernel,
        out_shape=[shp, shp, shp],
        grid=(nblk,),
        in_specs=[pl.BlockSpec((tp, qry.shape[1]), lambda i: (i, 0)),
                  pl.BlockSpec(sub_keys_bf16.shape, lambda i: (0, 0, 0, 0))],
        out_specs=[spec, spec, spec],
        scratch_shapes=[pltpu.VMEM((24, tp), F32), pltpu.VMEM((24, tp), F32)],
        compiler_params=_cparams(("parallel",)),
        name="peer_scores",
    )(qry, sub_keys_bf16)


PEER_SUB = 2
PEER_NSB = 8
PEER_TILE = 64


def _peer_expert_kernel(h_ref, u_ref, vt_ref, e2_ref, thr_ref, e1_ref, x_ref, gate_ref, o_ref, acc_ref, act_a,
                        act_b, coef_a, coef_b):
    act_ref = (act_a, act_b)
    coef_ref = (coef_a, coef_b)
    e = pl.program_id(1)

    @pl.when(e == 0)
    def _():
        acc_ref[...] = jnp.zeros_like(acc_ref)

    tp = h_ref.shape[0]
    width = PEER_SUB * PEER_KEYS
    inv_sqrt2 = 1.0 / math.sqrt(2.0)
    ntile = PEER_KEYS // PEER_TILE

    def scores(i, slot):
        act_ref[slot][...] = lax.dot_general(u_ref[i * width:(i + 1) * width, :], h_ref[...],
                                             (((1,), (1,)), ((), ())), preferred_element_type=F32)

    def combine(i, slot):
        acc_ref[...] += jnp.dot(vt_ref[:, i * width:(i + 1) * width], coef_ref[slot][...],
                                preferred_element_type=F32)

    def gates(i, slot):
        for tl in range(tp // LANES):
            ls = slice(tl * LANES, (tl + 1) * LANES)
            for r in range(PEER_SUB):
                row = i * PEER_SUB + r
                ms = [None] * ntile
                for hd in range(PEER_HEADS):
                    thr = thr_ref[0, hd, row:row + 1, ls]
                    e1 = e1_ref[0, hd, row:row + 1, ls]
                    for t2 in range(ntile):
                        e2 = e2_ref[0, hd, t2 * PEER_TILE:(t2 + 1) * PEER_TILE, ls]
                        term = jnp.where(e2 >= thr, e2, 0.0) * e1
                        ms[t2] = term if ms[t2] is None else ms[t2] + term
                for t2 in range(ntile):
                    lo = r * PEER_KEYS + t2 * PEER_TILE
                    a = act_ref[slot][lo:lo + PEER_TILE, ls]
                    coef_ref[slot][lo:lo + PEER_TILE, ls] = (
                        ms[t2] * (a * (1.0 + lax.erf(a * inv_sqrt2)))).astype(BF16)

    scores(0, 0)
    for i in range(PEER_NSB):
        if i + 1 < PEER_NSB:
            scores(i + 1, (i + 1) % 2)
        gates(i, i % 2)
        if i >= 1:
            combine(i - 1, (i - 1) % 2)
    combine(PEER_NSB - 1, (PEER_NSB - 1) % 2)

    @pl.when(e == pl.num_programs(1) - 1)
    def _():
        o_ref[...] = x_ref[...] + gate_ref[0] * acc_ref[...].T


def peer_experts(h_bf16, u_bf16, vt_bf16, e2, thr, e1, x, gate, *, tp, row_of_block, n_rows):
    n, d = x.shape
    n_exp = u_bf16.shape[0]
    rows = PEER_NSB * PEER_SUB
    width = PEER_SUB * PEER_KEYS
    ew = rows * PEER_KEYS
    sspec = pl.BlockSpec((1, PEER_HEADS, PEER_KEYS, tp), lambda i, e: (i, 0, 0, 0))
    rspec = pl.BlockSpec((1, PEER_HEADS, rows, tp), lambda i, e: (i, 0, e, 0))
    return pl.pallas_call(
        _peer_expert_kernel,
        out_shape=jax.ShapeDtypeStruct((n, d), F32),
        grid=(n_rows // tp, n_exp // ew),
        in_specs=[pl.BlockSpec((tp, d), lambda i, e: (i, 0)),
                  pl.BlockSpec((ew, d), lambda i, e: (e, 0)),
                  pl.BlockSpec((d, ew), lambda i, e: (0, e)),
                  sspec, rspec, rspec,
                  pl.BlockSpec((tp, d), lambda i, e: (i, 0)),
                  pl.BlockSpec((1, 1, d), lambda i, e: (row_of_block(i), 0, 0))],
        out_specs=pl.BlockSpec((tp, d), lambda i, e: (i, 0)),
        scratch_shapes=[pltpu.VMEM((d, tp), F32), pltpu.VMEM((width, tp), F32), pltpu.VMEM((width, tp), F32),
                        pltpu.VMEM((width, tp), BF16), pltpu.VMEM((width, tp), BF16)],
        compiler_params=_cparams(("parallel", "arbitrary")),
        name="peer_experts",
    )(h_bf16, u_bf16, vt_bf16, e2, thr, e1, x, gate)


def _final_norm_kernel(x_ref, g_ref, o_ref):
    x = x_ref[...]
    o_ref[...] = x * lax.rsqrt(jnp.mean(x * x, axis=-1, keepdims=True) + EPS) * g_ref[...]


def final_norm(x, gain, *, tm, n_rows):
    d = x.shape[1]
    return pl.pallas_call(
        _final_norm_kernel,
        out_shape=jax.ShapeDtypeStruct((n_rows, d), F32),
        grid=(n_rows // tm,),
        in_specs=[pl.BlockSpec((tm, d), lambda i: (i, 0)), pl.BlockSpec((1, d), lambda i: (0, 0))],
        out_specs=pl.BlockSpec((tm, d), lambda i: (i, 0)),
        compiler_params=_cparams(("parallel",)),
        name="final_norm",
    )(x, gain.reshape(1, d))


def kernel(x, c, ctx, c_ctx, ada_w, ada_b, norm1_g, norm2_g, final_g, dn_w_in, dn_conv_w, dn_a_log, dn_dt_bias,
           dn_norm_g, dn_w_out, att_w_in, att_qn_g, att_kn_g, att_w_out, peer_w_query, peer_sub_keys, peer_u,
           peer_v):
    nbatch, t_lat, d = x.shape
    t_ctx = ctx.shape[1]
    depth = ada_w.shape[0]
    n_lat = nbatch * t_lat
    n_all = n_lat + nbatch * t_ctx
    assert nbatch + 1 <= MOD_ROWS

    tm = min(512, t_ctx * nbatch, t_lat)
    ts = min(256, t_ctx)
    assert t_lat % tm == 0 and (nbatch * t_ctx) % tm == 0 and t_lat % ts == 0 and t_ctx % ts == 0
    assert t_lat % DN_CHUNK == 0 and t_ctx % DN_CHUNK == 0 and ts % DN_CHUNK == 0
    row_tm = _mod_row(n_lat // tm, t_lat // tm, nbatch)

    xs = jnp.concatenate([x.reshape(n_lat, d), ctx.reshape(nbatch * t_ctx, d)], axis=0)
    cond_rows = jnp.concatenate([c, c_ctx[None, :], jnp.zeros((MOD_ROWS - nbatch - 1, d), F32)], axis=0)
    mod = ada_all(cond_rows, ada_w, ada_b).reshape(depth, MOD_ROWS, N_MOD, 1, d)
    cos2, sin2 = rope_tables(t_lat, ts)

    dn_cols = dn_w_in.shape[2]
    dn_pad = (-dn_cols) % (7 * LANES)
    for i in range(depth):
        last = i == depth - 1
        sh1, sc1, g1, sh2, sc2, g2 = (mod[i, :, m] for m in range(N_MOD))
        j = i // 2
        n_act = n_lat if last else n_all
        if i % 2 == 0:
            w_in = jnp.pad(dn_w_in[j], ((0, 0), (0, dn_pad))).astype(BF16)
            proj = norm_mod_matmul(xs, norm1_g[i], sc1, sh1, w_in, tm=tm, tn=7 * LANES, row_of_block=row_tm)
            conv_w = dn_conv_w[j]
            qkv = dn_prep(proj, conv_w, tm=ts, nlat_blk=n_lat // ts, blk_per_lat=t_lat // ts,
                          blk_per_ctx=t_ctx // ts)
            conv_cols = conv_w.shape[1]
            val_w = DN_V_HEADS * HEAD_DIM
            rows, cols = dn_gates(proj, dn_a_log[j], dn_dt_bias[j], tm=ts,
                                  gate_col_block=(conv_cols + val_w) // LANES)
            o_f, o_b = dn_scan(qkv, rows, cols, nbatch=nbatch, t_lat=t_lat, t_ctx=t_ctx)
            xs = dn_out(o_f, o_b, proj, dn_norm_g[j], dn_w_out[j].astype(BF16), xs, g1, tm=tm,
                        row_of_block=row_tm, z_col_block=conv_cols // val_w)
        else:
            w_in = att_w_in[j].astype(BF16)
            proj = norm_mod_matmul(xs, norm1_g[i], sc1, sh1, w_in, tm=tm, tn=w_in.shape[1], row_of_block=row_tm)
            qn, kn, vn = att_prep(proj, att_qn_g[j], att_kn_g[j], cos2, sin2, tm=ts, nlat_blk=n_lat // ts,
                                  blk_per_lat=t_lat // ts)
            tq = min(128, t_ctx)
            o = attention(qn, kn, vn, nbatch=nbatch, t_lat=t_lat, t_ctx=t_ctx, tq=tq)
            if not last:
                o = attention(qn, kn, vn, o, nbatch=nbatch, t_lat=t_lat, t_ctx=t_ctx, tq=tq)
            xs = matmul_residual(o, att_w_out[j].astype(BF16), xs, g1, tm=tm, row_of_block=row_tm, n_rows=n_act)
        qry, h2 = norm_mod_matmul(xs, norm2_g[i], sc2, sh2, peer_w_query[i].astype(BF16), tm=tm,
                                  tn=peer_w_query.shape[2] // 2, row_of_block=row_tm, n_rows=n_act, with_h=True)
        e2, thr, e1 = peer_scores(qry, peer_sub_keys[i].astype(BF16), tp=tm, n_rows=n_act)
        xs = peer_experts(h2, peer_u[i].astype(BF16), peer_v[i].T.astype(BF16), e2, thr, e1, xs, g2, tp=tm,
                          row_of_block=row_tm, n_rows=n_act)
    return final_norm(xs, final_g, tm=tm, n_rows=n_lat).reshape(nbatch, t_lat, d)
```

```python
import functools
import math

import jax
import jax.numpy as jnp
from jax import lax
from jax.experimental import pallas as pl
from jax.experimental.pallas import tpu as pltpu

F32 = jnp.float32
BF16 = jnp.bfloat16

EPS = 1e-6
N_MOD = 6
MOD_ROWS = 24
LANES = 128
SUBLANES = 8

HEAD_DIM = 128
DN_QK_HEADS = 8
DN_V_HEADS = 16
DN_CONV = 5
DN_CHUNK = 128
DN_HEADS_PER_STEP = 4
ATT_Q_HEADS = 8
ATT_KV_HEADS = 2
ATT_GROUP = ATT_Q_HEADS // ATT_KV_HEADS
GRID_W = 64
ROPE_THETA = 10000.0
ATT_SCORE_SCALE = HEAD_DIM ** -0.5 * math.log2(math.e)
PEER_HEADS = 8
PEER_KEYS = 128
PEER_TOPK = 16

VMEM_LIMIT = 56 * 1024 * 1024


def _cparams(sem, flags=None):
    return pltpu.CompilerParams(dimension_semantics=sem, vmem_limit_bytes=VMEM_LIMIT, flags=flags)


def _bdot(a, b):
    return jnp.dot(a.astype(BF16), b.astype(BF16), preferred_element_type=F32)


def _bdot_nt(a, b):
    return lax.dot_general(a.astype(BF16), b.astype(BF16), (((1,), (1,)), ((), ())),
                           preferred_element_type=F32)


def _bdot_tn(a, b):
    return lax.dot_general(a.astype(BF16), b.astype(BF16), (((0,), (0,)), ((), ())),
                           preferred_element_type=F32)


def _split3(x):
    hi = x.astype(BF16)
    r = x - hi.astype(F32)
    mid = r.astype(BF16)
    lo = (r - mid.astype(F32)).astype(BF16)
    return hi, mid, lo


def _exact_dot_01(a01, x):
    a = a01.astype(BF16)
    hi, mid, lo = _split3(x)
    return (jnp.dot(a, hi, preferred_element_type=F32) + jnp.dot(a, mid, preferred_element_type=F32)
            + jnp.dot(a, lo, preferred_element_type=F32))


def _mod_row(nlat_blk, blk_per_batch, nbatch):
    def f(i):
        return jnp.where(i < nlat_blk, i // blk_per_batch, nbatch)
    return f


def _ada_kernel(c_ref, w_ref, b_ref, o_ref):
    c = c_ref[...]
    s = c * jax.nn.sigmoid(c)
    o_ref[0] = _bdot(s, w_ref[0]) + b_ref[0]


def ada_all(cond_rows, ada_w, ada_b):
    depth, d, n = ada_w.shape
    tn = 1536
    return pl.pallas_call(
        _ada_kernel,
        out_shape=jax.ShapeDtypeStruct((depth, MOD_ROWS, n), F32),
        grid=(depth, n // tn),
        in_specs=[pl.BlockSpec((MOD_ROWS, d), lambda l, j: (0, 0)),
                  pl.BlockSpec((1, d, tn), lambda l, j: (l, 0, j)),
                  pl.BlockSpec((1, 1, tn), lambda l, j: (l, 0, j))],
        out_specs=pl.BlockSpec((1, MOD_ROWS, tn), lambda l, j: (l, 0, j)),
        compiler_params=_cparams(("parallel", "parallel")),
        name="ada",
    )(cond_rows, ada_w, ada_b.reshape(depth, 1, n))


def _nmm_kernel(x_ref, g_ref, sc_ref, sh_ref, w_ref, o_ref, *rest, with_h):
    if with_h:
        h_ref, hs_ref = rest
    else:
        (hs_ref,) = rest

    @pl.when(pl.program_id(1) == 0)
    def _():
        x = x_ref[...]
        y = x * lax.rsqrt(jnp.mean(x * x, axis=-1, keepdims=True) + EPS) * g_ref[...]
        h = (y * (1.0 + sc_ref[0]) + sh_ref[0]).astype(BF16)
        hs_ref[...] = h
        if with_h:
            h_ref[...] = h

    o_ref[...] = jnp.dot(hs_ref[...], w_ref[...], preferred_element_type=F32)


def norm_mod_matmul(x, gain, scale, shift, w_bf16, *, tm, tn, row_of_block, n_rows=None, with_h=False):
    n, d = x.shape
    n_rows = n if n_rows is None else n_rows
    nout = w_bf16.shape[1]
    out_shape = [jax.ShapeDtypeStruct((n, nout), F32)]
    out_specs = [pl.BlockSpec((tm, tn), lambda i, j: (i, j))]
    if with_h:
        out_shape.append(jax.ShapeDtypeStruct((n, d), BF16))
        out_specs.append(pl.BlockSpec((tm, d), lambda i, j: (i, 0)))
    res = pl.pallas_call(
        functools.partial(_nmm_kernel, with_h=with_h),
        out_shape=out_shape,
        grid=(n_rows // tm, nout // tn),
        in_specs=[pl.BlockSpec((tm, d), lambda i, j: (i, 0)),
                  pl.BlockSpec((1, d), lambda i, j: (0, 0)),
                  pl.BlockSpec((1, 1, d), lambda i, j: (row_of_block(i), 0, 0)),
                  pl.BlockSpec((1, 1, d), lambda i, j: (row_of_block(i), 0, 0)),
                  pl.BlockSpec((d, tn), lambda i, j: (0, j))],
        out_specs=out_specs,
        scratch_shapes=[pltpu.VMEM((tm, d), BF16)],
        compiler_params=_cparams(("parallel", "arbitrary")),
        name="norm_mod_matmul",
    )(x, gain.reshape(1, d), scale, shift, w_bf16)
    return res if with_h else res[0]


def _mm_res_kernel(a_ref, w_ref, x_ref, gate_ref, o_ref):
    y = jnp.dot(a_ref[...].astype(BF16), w_ref[...], preferred_element_type=F32)
    o_ref[...] = x_ref[...] + gate_ref[0] * y


def matmul_residual(a, w_bf16, x, gate, *, tm, row_of_block, n_rows):
    n, d = x.shape
    k = a.shape[1]
    return pl.pallas_call(
        _mm_res_kernel,
        out_shape=jax.ShapeDtypeStruct((n, d), F32),
        grid=(n_rows // tm,),
        in_specs=[pl.BlockSpec((tm, k), lambda i: (i, 0)),
                  pl.BlockSpec((k, d), lambda i: (0, 0)),
                  pl.BlockSpec((tm, d), lambda i: (i, 0)),
                  pl.BlockSpec((1, 1, d), lambda i: (row_of_block(i), 0, 0))],
        out_specs=pl.BlockSpec((tm, d), lambda i: (i, 0)),
        compiler_params=_cparams(("parallel",)),
        name="matmul_residual",
    )(a, w_bf16, x, gate)


def _dn_prep_kernel(x_ref, xp_ref, xn_ref, w_ref, o_ref, *, tm, nlat_blk, blk_per_lat, blk_per_ctx):
    i = pl.program_id(0)
    j = pl.program_id(1)
    is_lat = i < nlat_blk
    pos = jnp.where(is_lat, i % blk_per_lat, (i - nlat_blk) % blk_per_ctx)
    nblk = jnp.where(is_lat, blk_per_lat, blk_per_ctx)
    x = x_ref[...]
    xp = jnp.where(pos > 0, xp_ref[...], 0.0)
    xn = jnp.where(pos < nblk - 1, xn_ref[...], 0.0)
    xe = jnp.concatenate([xp, x, xn], axis=0)
    half = DN_CONV // 2
    y = None
    for t in range(DN_CONV):
        lo = SUBLANES - half + t
        term = w_ref[t:t + 1, :] * xe[lo:lo + tm, :]
        y = term if y is None else y + term
    y = y * jax.nn.sigmoid(y)

    @pl.when(j < 2)
    def _():
        post = jnp.where(j == 0, HEAD_DIM ** -0.5, 1.0)
        for hh in range(y.shape[1] // HEAD_DIM):
            seg = y[:, hh * HEAD_DIM:(hh + 1) * HEAD_DIM]
            inv = lax.rsqrt(jnp.sum(seg * seg, axis=-1, keepdims=True) + EPS)
            o_ref[:, hh * HEAD_DIM:(hh + 1) * HEAD_DIM] = seg * inv * post

    @pl.when(j >= 2)
    def _():
        o_ref[...] = y


def dn_prep(proj, conv_w, *, tm, nlat_blk, blk_per_lat, blk_per_ctx):
    n = proj.shape[0]
    tc = DN_QK_HEADS * HEAD_DIM
    ncol = conv_w.shape[1] // tc
    sub = tm // SUBLANES
    last = n // SUBLANES - 1
    return pl.pallas_call(
        functools.partial(_dn_prep_kernel, tm=tm, nlat_blk=nlat_blk, blk_per_lat=blk_per_lat,
                          blk_per_ctx=blk_per_ctx),
        out_shape=jax.ShapeDtypeStruct((n, conv_w.shape[1]), F32),
        grid=(n // tm, ncol),
        in_specs=[pl.BlockSpec((tm, tc), lambda i, j: (i, j)),
                  pl.BlockSpec((SUBLANES, tc), lambda i, j: (jnp.maximum(i * sub - 1, 0), j)),
                  pl.BlockSpec((SUBLANES, tc), lambda i, j: (jnp.minimum((i + 1) * sub, last), j)),
                  pl.BlockSpec((DN_CONV, tc), lambda i, j: (0, j))],
        out_specs=pl.BlockSpec((tm, tc), lambda i, j: (i, j)),
        compiler_params=_cparams(("parallel", "parallel")),
        name="dn_prep",
    )(proj, proj, proj, conv_w)


def _dn_gates_kernel(x_ref, alog_ref, dtb_ref, perm_ref, rows_ref, cols_ref, *, tm):
    x = x_ref[...]
    lane = lax.broadcasted_iota(jnp.int32, x.shape, 1)
    is_a = (lane % 32) >= DN_V_HEADS
    is_bwd = lane >= 32
    beta = jax.nn.sigmoid(x)
    g = -jnp.exp(alog_ref[...]) * jax.nn.softplus(x + dtb_ref[...])
    c = DN_CHUNK
    ii = lax.broadcasted_iota(jnp.int32, (c, c), 0)
    jj = lax.broadcasted_iota(jnp.int32, (c, c), 1)
    lower = (ii >= jj).astype(F32)
    upper = (ii <= jj).astype(F32)
    chunk_is_bwd = lax.broadcasted_iota(jnp.int32, (c, x.shape[1]), 1) >= 32
    parts = []
    for ch in range(tm // c):
        gch = g[ch * c:(ch + 1) * c, :]
        parts.append(jnp.where(chunk_is_bwd, _exact_dot_01(upper, gch), _exact_dot_01(lower, gch)))
    gcum = jnp.concatenate(parts, axis=0) if len(parts) > 1 else parts[0]
    vals = jnp.where(is_a, gcum, beta)
    hi, mid, lo = _split3(vals)
    p = perm_ref[...]
    grouped = (jnp.dot(hi, p, preferred_element_type=F32) + jnp.dot(mid, p, preferred_element_type=F32)
               + jnp.dot(lo, p, preferred_element_type=F32))
    gt = grouped.T
    for hq in range(DN_QK_HEADS):
        cols_ref[hq] = grouped[:, hq * SUBLANES:(hq + 1) * SUBLANES]
        rows_ref[hq] = gt[hq * SUBLANES:(hq + 1) * SUBLANES, :]


def dn_gates(proj, a_log, dt_bias, *, tm, gate_col_block):
    n = proj.shape[0]
    zeros16 = jnp.zeros((2, DN_V_HEADS), F32)
    alog = jnp.pad(jnp.stack([zeros16, a_log], axis=1).reshape(1, 64), ((0, 0), (0, 64)))
    dtb = jnp.pad(jnp.stack([zeros16, dt_bias], axis=1).reshape(1, 64), ((0, 0), (0, 64)))
    src = jnp.arange(64)
    d_, kind, hv = src // 32, (src % 32) // 16, src % 16
    dst = (hv // 2) * 8 + d_ * 4 + kind * 2 + (hv % 2)
    perm = jnp.zeros((LANES, LANES), BF16).at[src, dst].set(1.0)
    return pl.pallas_call(
        functools.partial(_dn_gates_kernel, tm=tm),
        out_shape=[jax.ShapeDtypeStruct((DN_QK_HEADS, SUBLANES, n), F32),
                   jax.ShapeDtypeStruct((DN_QK_HEADS, n, SUBLANES), F32)],
        grid=(n // tm,),
        in_specs=[pl.BlockSpec((tm, LANES), lambda i: (i, gate_col_block)),
                  pl.BlockSpec((1, LANES), lambda i: (0, 0)),
                  pl.BlockSpec((1, LANES), lambda i: (0, 0)),
                  pl.BlockSpec((LANES, LANES), lambda i: (0, 0))],
        out_specs=[pl.BlockSpec((DN_QK_HEADS, SUBLANES, tm), lambda i: (0, 0, i)),
                   pl.BlockSpec((DN_QK_HEADS, tm, SUBLANES), lambda i: (0, i, 0))],
        compiler_params=_cparams(("parallel",)),
        name="dn_gates",
    )(proj, alog, dtb, perm)


def _dot_split(a, b):
    a_hi = a.astype(BF16)
    a_lo = (a - a_hi.astype(F32)).astype(BF16)
    b_hi = b.astype(BF16)
    b_lo = (b - b_hi.astype(F32)).astype(BF16)
    return (jnp.dot(a_hi, b_hi, preferred_element_type=F32) + jnp.dot(a_hi, b_lo, preferred_element_type=F32)
            + jnp.dot(a_lo, b_hi, preferred_element_type=F32))


TRI_BASE = 8


def _tri_inverse_minus_eye(nms, xr):
    c = nms[0].shape[0]
    k = int(math.log2(TRI_BASE))
    base = (xr >> k) == 0
    nds = [jnp.where(base, nm, 0.0) for nm in nms]
    xs = [-nd for nd in nds]
    pws = nds
    for _ in range(k - 1):
        pws = [_bdot(pw, pw) for pw in pws]
        xs = [x + pw + _bdot(x, pw) for x, pw in zip(xs, pws)]
    while (1 << k) < c:
        sib = (xr >> k) == 1
        cms = [jnp.where(sib, nm, 0.0) for nm in nms]
        ts = [cm + _bdot(x, cm) for x, cm in zip(xs, cms)]
        xs = [x - (t + _bdot(t, x)) for x, t in zip(xs, ts)]
        k += 1
    return xs


def _delta_chunks(qs, ks, chains, c):
    ii = lax.broadcasted_iota(jnp.int32, (c, c), 0)
    jj = lax.broadcasted_iota(jnp.int32, (c, c), 1)
    xr = ii ^ jj
    incl = (ii >= jj, ii <= jj)
    strict = (ii > jj, ii < jj)
    kks = [_bdot_nt(k, k) for k in ks]
    qks = [_bdot_nt(q, k) for q, k in zip(qs, ks)]
    decays = [jnp.exp(jnp.where(incl[ch["d"]], ch["gcol"] - ch["grow"], -jnp.inf)) for ch in chains]
    nms = [jnp.where(strict[ch["d"]], kks[ch["qk"]] * dec * ch["bcol"], 0.0) for ch, dec in zip(chains, decays)]
    xs = _tri_inverse_minus_eye(nms, xr)
    egcs = [jnp.exp(ch["gcol"]) for ch in chains]
    rhss = [jnp.concatenate([ch["v"] * ch["bcol"], ks[ch["qk"]] * (ch["bcol"] * egc)], axis=1)
            for ch, egc in zip(chains, egcs)]
    sols = [rhs + _bdot(x, rhs) for x, rhs in zip(xs, rhss)]
    dv = chains[0]["v"].shape[1]
    v_news = [sol[:, :dv] - _bdot(sol[:, dv:], ch["s"]) for sol, ch in zip(sols, chains)]
    qkms = [jnp.where(incl[ch["d"]], qks[ch["qk"]] * dec, 0.0) for ch, dec in zip(chains, decays)]
    outs = [_bdot(qs[ch["qk"]] * egc, ch["s"]) + _bdot(qkm, vn)
            for ch, egc, qkm, vn in zip(chains, egcs, qkms, v_news)]
    g_lasts = [ch["gcol"][c - 1:c, :] if ch["d"] == 0 else ch["gcol"][0:1, :] for ch in chains]
    s_news = [ch["s"] * jnp.exp(gl) + _bdot_tn(ks[ch["qk"]] * jnp.exp(gl - ch["gcol"]), vn)
              for ch, gl, vn in zip(chains, g_lasts, v_news)]
    return outs, s_news


def _dn_scan_kernel(qf_ref, kf_ref, vf_ref, rf_ref, cf_ref, qb_ref, kb_ref, vb_ref, rb_ref, cb_ref,
                    of_ref, ob_ref, s_ref):
    @pl.when(pl.program_id(2) == 0)
    def _():
        s_ref[...] = jnp.zeros_like(s_ref)

    rep = DN_V_HEADS // DN_QK_HEADS
    c = qf_ref.shape[0]
    hps = qf_ref.shape[1] // HEAD_DIM
    dirs = ((qf_ref, kf_ref, vf_ref, rf_ref, cf_ref), (qb_ref, kb_ref, vb_ref, rb_ref, cb_ref))
    qs, ks, chains = [], [], []
    for d, (q_ref, k_ref, v_ref, r_ref, c_ref) in enumerate(dirs):
        for hh in range(hps):
            qs.append(q_ref[:, hh * HEAD_DIM:(hh + 1) * HEAD_DIM])
            ks.append(k_ref[:, hh * HEAD_DIM:(hh + 1) * HEAD_DIM])
            rows = r_ref[hh]
            cols = c_ref[hh]
            for r in range(rep):
                ib = d * 4 + r
                ig = d * 4 + 2 + r
                col = (hh * rep + r) * HEAD_DIM
                chains.append(dict(d=d, qk=d * hps + hh, col=col, v=v_ref[:, col:col + HEAD_DIM],
                                   bcol=cols[:, ib:ib + 1], gcol=cols[:, ig:ig + 1], grow=rows[ig:ig + 1, :],
                                   s=s_ref[len(chains)]))
    outs, s_news = _delta_chunks(qs, ks, chains, c)
    for i, (ch, o, s_new) in enumerate(zip(chains, outs, s_news)):
        (of_ref, ob_ref)[ch["d"]][:, ch["col"]:ch["col"] + HEAD_DIM] = o
        s_ref[i] = s_new


def dn_scan(qkv, rows, cols, *, nbatch, t_lat, t_ctx):
    n = qkv.shape[0]
    c = DN_CHUNK
    nc, nl = t_ctx // c, t_lat // c
    lat0 = lambda b: b * nl
    ctx0 = lambda b: (nbatch * t_lat) // c + b * nc

    def fwd(b, s):
        return jnp.where(s < nc, ctx0(b) + s, lat0(b) + (s - nc))

    def bwd(b, s):
        return jnp.where(s < nc, ctx0(b) + (nc - 1 - s), lat0(b) + (nl - 1 - (s - nc)))

    hps = DN_HEADS_PER_STEP
    rep = DN_V_HEADS // DN_QK_HEADS
    qw = HEAD_DIM * hps
    vw = qw * rep
    k_col0 = DN_QK_HEADS // hps
    v_col0 = 2 * DN_QK_HEADS * HEAD_DIM // vw

    def specs(blk):
        return [pl.BlockSpec((c, qw), lambda b, h, s: (blk(b, s), h)),
                pl.BlockSpec((c, qw), lambda b, h, s: (blk(b, s), k_col0 + h)),
                pl.BlockSpec((c, vw), lambda b, h, s: (blk(b, s), v_col0 + h)),
                pl.BlockSpec((hps, SUBLANES, c), lambda b, h, s: (h, 0, blk(b, s))),
                pl.BlockSpec((hps, c, SUBLANES), lambda b, h, s: (h, blk(b, s), 0))]

    ow = DN_V_HEADS * HEAD_DIM
    return pl.pallas_call(
        _dn_scan_kernel,
        out_shape=[jax.ShapeDtypeStruct((n, ow), F32), jax.ShapeDtypeStruct((n, ow), F32)],
        grid=(nbatch, DN_QK_HEADS // hps, nc + nl),
        in_specs=specs(fwd) + specs(bwd),
        out_specs=[pl.BlockSpec((c, vw), lambda b, h, s: (fwd(b, s), h)),
                   pl.BlockSpec((c, vw), lambda b, h, s: (bwd(b, s), h))],
        scratch_shapes=[pltpu.VMEM((2 * hps * rep, HEAD_DIM, HEAD_DIM), F32)],
        compiler_params=_cparams(("parallel", "parallel", "arbitrary")),
        name="dn_scan",
    )(qkv, qkv, qkv, rows, cols, qkv, qkv, qkv, rows, cols)


def _dn_out_kernel(of_ref, ob_ref, z_ref, ng_ref, w_ref, x_ref, gate_ref, o_ref):
    o = of_ref[...] + ob_ref[...]
    z = z_ref[...]
    parts = []
    for hh in range(DN_V_HEADS):
        seg = o[:, hh * HEAD_DIM:(hh + 1) * HEAD_DIM]
        y = seg * lax.rsqrt(jnp.mean(seg * seg, axis=-1, keepdims=True) + EPS) * ng_ref[...]
        zz = z[:, hh * HEAD_DIM:(hh + 1) * HEAD_DIM]
        parts.append((y * (zz * jax.nn.sigmoid(zz))).astype(BF16))
    a = jnp.concatenate(parts, axis=1)
    o_ref[...] = x_ref[...] + gate_ref[0] * jnp.dot(a, w_ref[...], preferred_element_type=F32)


def dn_out(o_f, o_b, proj, norm_g, w_bf16, x, gate, *, tm, row_of_block, z_col_block):
    n, d = x.shape
    k = o_f.shape[1]
    return pl.pallas_call(
        _dn_out_kernel,
        out_shape=jax.ShapeDtypeStruct((n, d), F32),
        grid=(n // tm,),
        in_specs=[pl.BlockSpec((tm, k), lambda i: (i, 0)),
                  pl.BlockSpec((tm, k), lambda i: (i, 0)),
                  pl.BlockSpec((tm, k), lambda i: (i, z_col_block)),
                  pl.BlockSpec((1, HEAD_DIM), lambda i: (0, 0)),
                  pl.BlockSpec((k, d), lambda i: (0, 0)),
                  pl.BlockSpec((tm, d), lambda i: (i, 0)),
                  pl.BlockSpec((1, 1, d), lambda i: (row_of_block(i), 0, 0))],
        out_specs=pl.BlockSpec((tm, d), lambda i: (i, 0)),
        compiler_params=_cparams(("parallel",)),
        name="dn_out",
    )(o_f, o_b, proj, norm_g.reshape(1, HEAD_DIM), w_bf16, x, gate)


def _swap_pairs(x):
    lane = lax.broadcasted_iota(jnp.int32, x.shape, 1)
    nxt = pltpu.roll(x, x.shape[1] - 1, 1)
    prv = pltpu.roll(x, 1, 1)
    return jnp.where(lane % 2 == 0, nxt, prv)


def _att_prep_kernel(p_ref, qg_ref, kg_ref, cos_ref, sin_ref, q_ref, k_ref, v_ref):
    cos = cos_ref[...]
    sin = sin_ref[...]

    def norm_rope(seg, gain, post=1.0):
        y = seg * lax.rsqrt(jnp.mean(seg * seg, axis=-1, keepdims=True) + EPS) * gain
        return ((y * cos + _swap_pairs(y) * sin) * post).astype(BF16)

    qw = ATT_Q_HEADS * HEAD_DIM
    for hh in range(ATT_Q_HEADS):
        q_ref[:, hh * HEAD_DIM:(hh + 1) * HEAD_DIM] = norm_rope(
            p_ref[:, hh * HEAD_DIM:(hh + 1) * HEAD_DIM], qg_ref[...], ATT_SCORE_SCALE)
    for hh in range(ATT_KV_HEADS):
        k_ref[:, hh * HEAD_DIM:(hh + 1) * HEAD_DIM] = norm_rope(
            p_ref[:, qw + hh * HEAD_DIM:qw + (hh + 1) * HEAD_DIM], kg_ref[...])
    kvw = ATT_KV_HEADS * HEAD_DIM
    v_ref[...] = p_ref[:, qw + kvw:qw + 2 * kvw].astype(BF16)


def att_prep(proj, qn_g, kn_g, cos2, sin2, *, tm, nlat_blk, blk_per_lat):
    n, pw = proj.shape
    qw = ATT_Q_HEADS * HEAD_DIM
    kvw = ATT_KV_HEADS * HEAD_DIM
    ident_blk = blk_per_lat
    tab = lambda i: (jnp.where(i < nlat_blk, i % blk_per_lat, ident_blk), 0)
    return pl.pallas_call(
        _att_prep_kernel,
        out_shape=[jax.ShapeDtypeStruct((n, qw), BF16), jax.ShapeDtypeStruct((n, kvw), BF16),
                   jax.ShapeDtypeStruct((n, kvw), BF16)],
        grid=(n // tm,),
        in_specs=[pl.BlockSpec((tm, pw), lambda i: (i, 0)),
                  pl.BlockSpec((1, HEAD_DIM), lambda i: (0, 0)),
                  pl.BlockSpec((1, HEAD_DIM), lambda i: (0, 0)),
                  pl.BlockSpec((tm, HEAD_DIM), tab),
                  pl.BlockSpec((tm, HEAD_DIM), tab)],
        out_specs=[pl.BlockSpec((tm, qw), lambda i: (i, 0)),
                   pl.BlockSpec((tm, kvw), lambda i: (i, 0)),
                   pl.BlockSpec((tm, kvw), lambda i: (i, 0))],
        compiler_params=_cparams(("parallel",)),
        name="att_prep",
    )(proj, qn_g.reshape(1, HEAD_DIM), kn_g.reshape(1, HEAD_DIM), cos2, sin2)


def rope_tables(t_lat, tm):
    rows = t_lat // GRID_W
    row = jnp.broadcast_to(jnp.arange(rows)[:, None], (rows, GRID_W)).reshape(-1).astype(F32)
    col = jnp.broadcast_to(jnp.arange(GRID_W)[None, :], (rows, GRID_W)).reshape(-1).astype(F32)
    axis_dim = HEAD_DIM // 2
    freqs = ROPE_THETA ** (-jnp.arange(0, axis_dim, 2, dtype=F32) / axis_dim)
    ang = jnp.concatenate([row[:, None] * freqs, col[:, None] * freqs], axis=-1)
    cos, sin = jnp.cos(ang), jnp.sin(ang)
    cos2 = jnp.repeat(cos, 2, axis=-1)
    sin2 = jnp.stack([-sin, sin], axis=-1).reshape(t_lat, HEAD_DIM)
    cos2 = jnp.concatenate([cos2, jnp.ones((tm, HEAD_DIM), F32)], axis=0)
    sin2 = jnp.concatenate([sin2, jnp.zeros((tm, HEAD_DIM), F32)], axis=0)
    return cos2, sin2


def _attn_kernel(q_ref, kc_ref, vc_ref, *rest, with_lat, kchunk):
    if with_lat:
        kl_ref, vl_ref, o_ref = rest
    else:
        _, o_ref = rest
    tq = q_ref.shape[0]
    q = jnp.concatenate([q_ref[:, r * HEAD_DIM:(r + 1) * HEAD_DIM] for r in range(ATT_GROUP)], axis=0)
    pieces = [(kc_ref, vc_ref, 0, kc_ref.shape[0])]
    if with_lat:
        for c0 in range(0, kl_ref.shape[0], kchunk):
            pieces.append((kl_ref, vl_ref, c0, kchunk))
    m = l = acc = None
    for k_ref, v_ref, c0, cn in pieces:
        s = lax.dot_general(q, k_ref[c0:c0 + cn, :], (((1,), (1,)), ((), ())),
                            preferred_element_type=F32)
        mx = jnp.max(s, axis=-1, keepdims=True)
        if m is None:
            m = mx
            p = jnp.exp2(s - m)
            l = jnp.sum(p, axis=-1, keepdims=True)
            acc = jnp.dot(p.astype(BF16), v_ref[c0:c0 + cn, :], preferred_element_type=F32)
        else:
            m_new = jnp.maximum(m, mx)
            alpha = jnp.exp2(m - m_new)
            p = jnp.exp2(s - m_new)
            l = alpha * l + jnp.sum(p, axis=-1, keepdims=True)
            acc = alpha * acc + jnp.dot(p.astype(BF16), v_ref[c0:c0 + cn, :], preferred_element_type=F32)
            m = m_new
    o = acc / l
    for r in range(ATT_GROUP):
        o_ref[:, r * HEAD_DIM:(r + 1) * HEAD_DIM] = o[r * tq:(r + 1) * tq, :].astype(BF16)


def attention(qn, kn, vn, o_lat=None, *, nbatch, t_lat, t_ctx, tq):
    latent_queries = o_lat is None
    n = qn.shape[0]
    gw = ATT_GROUP * HEAD_DIM
    lat_rows = nbatch * t_lat
    ctx_blk = lambda b: (lat_rows // t_ctx) + b
    if latent_queries:
        qblk = lambda b, qi: b * (t_lat // tq) + qi
        nq = t_lat // tq
    else:
        qblk = lambda b, qi: lat_rows // tq + b * (t_ctx // tq) + qi
        nq = t_ctx // tq
    in_specs = [pl.BlockSpec((tq, gw), lambda b, g, qi: (qblk(b, qi), g)),
                pl.BlockSpec((t_ctx, HEAD_DIM), lambda b, g, qi: (ctx_blk(b), g)),
                pl.BlockSpec((t_ctx, HEAD_DIM), lambda b, g, qi: (ctx_blk(b), g))]
    args = [qn, kn, vn]
    if latent_queries:
        in_specs += [pl.BlockSpec((t_lat, HEAD_DIM), lambda b, g, qi: (b, g)),
                     pl.BlockSpec((t_lat, HEAD_DIM), lambda b, g, qi: (b, g))]
        args += [kn, vn]
        aliases = {}
    else:
        in_specs.append(pl.BlockSpec(memory_space=pl.ANY))
        args.append(o_lat)
        aliases = {len(args) - 1: 0}
    kchunk = min(t_lat, 1024)
    return pl.pallas_call(
        functools.partial(_attn_kernel, with_lat=latent_queries, kchunk=kchunk),
        out_shape=jax.ShapeDtypeStruct((n, ATT_Q_HEADS * HEAD_DIM), BF16),
        grid=(nbatch, ATT_KV_HEADS, nq),
        in_specs=in_specs,
        out_specs=pl.BlockSpec((tq, gw), lambda b, g, qi: (qblk(b, qi), g)),
        input_output_aliases=aliases,
        compiler_params=_cparams(("parallel", "parallel", "arbitrary")),
        name="attention_lat" if latent_queries else "attention_ctx",
    )(*args)


N_CAND = PEER_TOPK + 1


def _sorting_network(n):
    pairs = []
    p = 1
    while p < n:
        k = p
        while k >= 1:
            for j in range(k % p, n - k, 2 * k):
                for i in range(min(k, n - j - k)):
                    if (i + j) // (2 * p) == (i + j + k) // (2 * p):
                        pairs.append((i + j, i + j + k))
            k //= 2
        p *= 2
    return pairs


def _top_values(ref, s, n):
    groups = s.shape[0] // SUBLANES
    v = [s[g * SUBLANES:(g + 1) * SUBLANES, :] for g in range(groups)]
    for a, b in _sorting_network(groups):
        hi = jnp.maximum(v[a], v[b])
        lo = jnp.minimum(v[a], v[b])
        v[a], v[b] = hi, lo
    neg = jnp.full_like(v[0], -jnp.inf)
    for r in range(n):
        m = jnp.max(v[0], axis=0, keepdims=True)
        ref[r:r + 1, :] = m
        pop = v[0] >= m
        keep = min(groups, n - 1 - r)
        for j in range(keep):
            nxt = v[j + 1] if j + 1 < groups else neg
            v[j] = jnp.where(pop, nxt, v[j])


def _peer_scores_kernel(q_ref, sk_ref, e2_ref, thr_ref, e1_ref, a_ref, b_ref):
    p = q_ref.shape[0]
    neg = -jnp.inf
    a_ref[...] = jnp.full(a_ref.shape, neg, F32)
    b_ref[...] = jnp.full(b_ref.shape, neg, F32)
    for h in range(PEER_HEADS):
        s1 = _bdot_nt(sk_ref[h, 0], q_ref[:, (2 * h) * PEER_KEYS:(2 * h + 1) * PEER_KEYS])
        s2 = _bdot_nt(sk_ref[h, 1], q_ref[:, (2 * h + 1) * PEER_KEYS:(2 * h + 2) * PEER_KEYS])
        _top_values(a_ref, s1, N_CAND)
        _top_values(b_ref, s2, N_CAND)
        a = a_ref[...]
        b = b_ref[...]
        a0, b0 = a[0:1, :], b[0:1, :]
        cands = [a0 + b, a[1:N_CAND, :] + b0]
        ridx = lax.broadcasted_iota(jnp.int32, (SUBLANES, p), 0)
        for i in range(1, SUBLANES):
            nj = N_CAND // (i + 1) - 1
            if nj < 1:
                break
            cands.append(jnp.where(ridx < nj, a[i:i + 1, :] + b[1:1 + SUBLANES, :], neg))
        cand = jnp.concatenate(cands, axis=0)
        cur = cand
        c_prev = None
        for r in range(N_CAND):
            m = jnp.max(cur, axis=0, keepdims=True)
            if r == N_CAND - 2:
                c_prev = m
            cur = jnp.where(cur >= m, neg, cur)
        tau = 0.5 * (c_prev + m)
        top = a0 + b0
        z = jnp.sum(jnp.where(cand > tau, jnp.exp(cand - top), 0.0), axis=0, keepdims=True)
        e2_ref[0, h] = jnp.exp(s2 - b0)
        thr_ref[0, h] = jnp.exp((tau - b0) - s1)
        e1_ref[0, h] = jnp.exp(s1 - a0) * (0.5 / z)


def peer_scores(qry, sub_keys_bf16, *, tp, n_rows):
    nblk = n_rows // tp
    shp = jax.ShapeDtypeStruct((nblk, PEER_HEADS, PEER_KEYS, tp), F32)
    spec = pl.BlockSpec((1, PEER_HEADS, PEER_KEYS, tp), lambda i: (i, 0, 0, 0))
    return pl.pallas_call(
        _peer_scores_kernel,
        out_shape=[shp, shp, shp],
        grid=(nblk,),
        in_specs=[pl.BlockSpec((tp, qry.shape[1]), lambda i: (i, 0)),
                  pl.BlockSpec(sub_keys_bf16.shape, lambda i: (0, 0, 0, 0))],
        out_specs=[spec, spec, spec],
        scratch_shapes=[pltpu.VMEM((24, tp), F32), pltpu.VMEM((24, tp), F32)],
        compiler_params=_cparams(("parallel",)),
        name="peer_scores",
    )(qry, sub_keys_bf16)


PEER_SUB = 2
PEER_NSB = 16
PEER_TILE = 64


def _peer_expert_kernel(h_ref, u_ref, vt_ref, e2_ref, thr_ref, e1_ref, x_ref, gate_ref, o_ref, acc_ref, act_a,
                        act_b, coef_a, coef_b):
    act_ref = (act_a, act_b)
    coef_ref = (coef_a, coef_b)
    e = pl.program_id(1)

    @pl.when(e == 0)
    def _():
        acc_ref[...] = jnp.zeros_like(acc_ref)

    tp = h_ref.shape[0]
    width = PEER_SUB * PEER_KEYS
    inv_sqrt2 = 1.0 / math.sqrt(2.0)
    ntile = PEER_KEYS // PEER_TILE

    def scores(i, slot):
        act_ref[slot][...] = lax.dot_general(u_ref[i * width:(i + 1) * width, :], h_ref[...],
                                             (((1,), (1,)), ((), ())), preferred_element_type=F32)

    def combine(i, slot):
        acc_ref[...] += jnp.dot(vt_ref[:, i * width:(i + 1) * width], coef_ref[slot][...],
                                preferred_element_type=F32)

    def gates(i, slot):
        for tl in range(tp // LANES):
            ls = slice(tl * LANES, (tl + 1) * LANES)
            for r in range(PEER_SUB):
                row = i * PEER_SUB + r
                ms = [None] * ntile
                for hd in range(PEER_HEADS):
                    thr = thr_ref[0, hd, row:row + 1, ls]
                    e1 = e1_ref[0, hd, row:row + 1, ls]
                    for t2 in range(ntile):
                        e2 = e2_ref[0, hd, t2 * PEER_TILE:(t2 + 1) * PEER_TILE, ls]
                        term = jnp.where(e2 >= thr, e2, 0.0) * e1
                        ms[t2] = term if ms[t2] is None else ms[t2] + term
                for t2 in range(ntile):
                    lo = r * PEER_KEYS + t2 * PEER_TILE
                    a = act_ref[slot][lo:lo + PEER_TILE, ls]
                    coef_ref[slot][lo:lo + PEER_TILE, ls] = (
                        ms[t2] * (a * (1.0 + lax.erf(a * inv_sqrt2)))).astype(BF16)

    scores(0, 0)
    for i in range(PEER_NSB):
        if i + 1 < PEER_NSB:
            scores(i + 1, (i + 1) % 2)
        gates(i, i % 2)
        if i >= 1:
            combine(i - 1, (i - 1) % 2)
    combine(PEER_NSB - 1, (PEER_NSB - 1) % 2)

    @pl.when(e == pl.num_programs(1) - 1)
    def _():
        o_ref[...] = x_ref[...] + gate_ref[0] * acc_ref[...].T


def peer_experts(h_bf16, u_bf16, vt_bf16, e2, thr, e1, x, gate, *, tp, row_of_block, n_rows):
    n, d = x.shape
    n_exp = u_bf16.shape[0]
    rows = PEER_NSB * PEER_SUB
    width = PEER_SUB * PEER_KEYS
    ew = rows * PEER_KEYS
    sspec = pl.BlockSpec((1, PEER_HEADS, PEER_KEYS, tp), lambda i, e: (i, 0, 0, 0))
    rspec = pl.BlockSpec((1, PEER_HEADS, rows, tp), lambda i, e: (i, 0, e, 0))
    return pl.pallas_call(
        _peer_expert_kernel,
        out_shape=jax.ShapeDtypeStruct((n, d), F32),
        grid=(n_rows // tp, n_exp // ew),
        in_specs=[pl.BlockSpec((tp, d), lambda i, e: (i, 0)),
                  pl.BlockSpec((ew, d), lambda i, e: (e, 0)),
                  pl.BlockSpec((d, ew), lambda i, e: (0, e)),
                  sspec, rspec, rspec,
                  pl.BlockSpec((tp, d), lambda i, e: (i, 0)),
                  pl.BlockSpec((1, 1, d), lambda i, e: (row_of_block(i), 0, 0))],
        out_specs=pl.BlockSpec((tp, d), lambda i, e: (i, 0)),
        scratch_shapes=[pltpu.VMEM((d, tp), F32), pltpu.VMEM((width, tp), F32), pltpu.VMEM((width, tp), F32),
                        pltpu.VMEM((width, tp), BF16), pltpu.VMEM((width, tp), BF16)],
        compiler_params=_cparams(("parallel", "arbitrary")),
        name="peer_experts",
    )(h_bf16, u_bf16, vt_bf16, e2, thr, e1, x, gate)


def _final_norm_kernel(x_ref, g_ref, o_ref):
    x = x_ref[...]
    o_ref[...] = x * lax.rsqrt(jnp.mean(x * x, axis=-1, keepdims=True) + EPS) * g_ref[...]


def final_norm(x, gain, *, tm, n_rows):
    d = x.shape[1]
    return pl.pallas_call(
        _final_norm_kernel,
        out_shape=jax.ShapeDtypeStruct((n_rows, d), F32),
        grid=(n_rows // tm,),
        in_specs=[pl.BlockSpec((tm, d), lambda i: (i, 0)), pl.BlockSpec((1, d), lambda i: (0, 0))],
        out_specs=pl.BlockSpec((tm, d), lambda i: (i, 0)),
        compiler_params=_cparams(("parallel",)),
        name="final_norm",
    )(x, gain.reshape(1, d))


def kernel(x, c, ctx, c_ctx, ada_w, ada_b, norm1_g, norm2_g, final_g, dn_w_in, dn_conv_w, dn_a_log, dn_dt_bias,
           dn_norm_g, dn_w_out, att_w_in, att_qn_g, att_kn_g, att_w_out, peer_w_query, peer_sub_keys, peer_u,
           peer_v):
    nbatch, t_lat, d = x.shape
    t_ctx = ctx.shape[1]
    depth = ada_w.shape[0]
    n_lat = nbatch * t_lat
    n_all = n_lat + nbatch * t_ctx
    assert nbatch + 1 <= MOD_ROWS

    tm = min(512, t_ctx * nbatch, t_lat)
    ts = min(256, t_ctx)
    assert t_lat % tm == 0 and (nbatch * t_ctx) % tm == 0 and t_lat % ts == 0 and t_ctx % ts == 0
    assert t_lat % DN_CHUNK == 0 and t_ctx % DN_CHUNK == 0 and ts % DN_CHUNK == 0
    row_tm = _mod_row(n_lat // tm, t_lat // tm, nbatch)

    xs = jnp.concatenate([x.reshape(n_lat, d), ctx.reshape(nbatch * t_ctx, d)], axis=0)
    cond_rows = jnp.concatenate([c, c_ctx[None, :], jnp.zeros((MOD_ROWS - nbatch - 1, d), F32)], axis=0)
    mod = ada_all(cond_rows, ada_w, ada_b).reshape(depth, MOD_ROWS, N_MOD, 1, d)
    cos2, sin2 = rope_tables(t_lat, ts)

    dn_cols = dn_w_in.shape[2]
    dn_pad = (-dn_cols) % (7 * LANES)
    for i in range(depth):
        last = i == depth - 1
        sh1, sc1, g1, sh2, sc2, g2 = (mod[i, :, m] for m in range(N_MOD))
        j = i // 2
        n_act = n_lat if last else n_all
        if i % 2 == 0:
            w_in = jnp.pad(dn_w_in[j], ((0, 0), (0, dn_pad))).astype(BF16)
            proj = norm_mod_matmul(xs, norm1_g[i], sc1, sh1, w_in, tm=tm, tn=7 * LANES, row_of_block=row_tm)
            conv_w = dn_conv_w[j]
            qkv = dn_prep(proj, conv_w, tm=ts, nlat_blk=n_lat // ts, blk_per_lat=t_lat // ts,
                          blk_per_ctx=t_ctx // ts)
            conv_cols = conv_w.shape[1]
            val_w = DN_V_HEADS * HEAD_DIM
            rows, cols = dn_gates(proj, dn_a_log[j], dn_dt_bias[j], tm=ts,
                                  gate_col_block=(conv_cols + val_w) // LANES)
            o_f, o_b = dn_scan(qkv, rows, cols, nbatch=nbatch, t_lat=t_lat, t_ctx=t_ctx)
            xs = dn_out(o_f, o_b, proj, dn_norm_g[j], dn_w_out[j].astype(BF16), xs, g1, tm=tm,
                        row_of_block=row_tm, z_col_block=conv_cols // val_w)
        else:
            w_in = att_w_in[j].astype(BF16)
            proj = norm_mod_matmul(xs, norm1_g[i], sc1, sh1, w_in, tm=tm, tn=w_in.shape[1], row_of_block=row_tm)
            qn, kn, vn = att_prep(proj, att_qn_g[j], att_kn_g[j], cos2, sin2, tm=ts, nlat_blk=n_lat // ts,
                                  blk_per_lat=t_lat // ts)
            tq = min(256, t_ctx)
            o = attention(qn, kn, vn, nbatch=nbatch, t_lat=t_lat, t_ctx=t_ctx, tq=tq)
            if not last:
                o = attention(qn, kn, vn, o, nbatch=nbatch, t_lat=t_lat, t_ctx=t_ctx, tq=tq)
            xs = matmul_residual(o, att_w_out[j].astype(BF16), xs, g1, tm=tm, row_of_block=row_tm, n_rows=n_act)
        qry, h2 = norm_mod_matmul(xs, norm2_g[i], sc2, sh2, peer_w_query[i].astype(BF16), tm=tm,
                                  tn=peer_w_query.shape[2] // 2, row_of_block=row_tm, n_rows=n_act, with_h=True)
        e2, thr, e1 = peer_scores(qry, peer_sub_keys[i].astype(BF16), tp=tm, n_rows=n_act)
        xs = peer_experts(h2, peer_u[i].astype(BF16), peer_v[i].T.astype(BF16), e2, thr, e1, xs, g2, tp=tm,
                          row_of_block=row_tm, n_rows=n_act)
    return final_norm(xs, final_g, tm=tm, n_rows=n_lat).reshape(nbatch, t_lat, d)
```
